```python
import jax, jax.numpy as jnp
from jax import lax
import numpy as np

D_MODEL = 1024
BATCH = 1
SEQ = 16384
DEPTH = 1

HEAD_DIM = 64
N_HEADS = D_MODEL // HEAD_DIM
N_SB_HEADS = N_HEADS // 2
N_FOX_HEADS = N_HEADS - N_SB_HEADS
SB_WIDTH = N_SB_HEADS * HEAD_DIM
FOX_WIDTH = N_FOX_HEADS * HEAD_DIM
IN_WIDTH = 3 * SB_WIDTH + 3 * FOX_WIDTH + N_FOX_HEADS
D_FF = ((-(-8 * D_MODEL // 3) + 255) // 256) * 256
BLOCK_Q = 128
DEEPNORM_ALPHA = (2 * DEPTH) ** 0.25
DEEPNORM_BETA = (8 * DEPTH) ** -0.25
LN_EPS = 1e-5
RMS_EPS = 1e-6

kernel_name = "hybrid_stickbreaking_forgetting_deepnorm"


def layer_norm(x, g, b):
    xf = x.astype(jnp.float32)
    mu = jnp.mean(xf, axis=-1, keepdims=True)
    var = jnp.mean(jnp.square(xf - mu), axis=-1, keepdims=True)
    return ((xf - mu) * lax.rsqrt(var + LN_EPS) * g + b).astype(x.dtype)


def head_rmsnorm(o, g):
    B, H, S, d = o.shape
    of = o.astype(jnp.float32)
    of = of * lax.rsqrt(jnp.mean(jnp.square(of), axis=-1, keepdims=True) + RMS_EPS)
    of = of * g.reshape(1, H, 1, d).astype(jnp.float32)
    return of.transpose(0, 2, 1, 3).reshape(B, S, H * d).astype(o.dtype)


def stick_breaking_attention(q, k, v):
    B, H, S, d = q.shape
    nb = S // BLOCK_Q
    scale = d ** -0.5
    qb = q.reshape(B, H, nb, BLOCK_Q, d).transpose(2, 0, 1, 3, 4)
    k_pos = jnp.arange(S)

    def block(args):
        qi, i = args
        z = jnp.einsum('bhqd,bhkd->bhqk', qi, k, preferred_element_type=jnp.float32) * scale
        q_pos = i * BLOCK_Q + jnp.arange(BLOCK_Q)
        mask = k_pos[None, :] < q_pos[:, None]
        log_keep = jnp.where(mask, jax.nn.log_sigmoid(-z), 0.0)
        log_after = lax.cumsum(log_keep, axis=3, reverse=True) - log_keep
        w = jnp.where(mask, jnp.exp(jax.nn.log_sigmoid(z) + log_after), 0.0)
        return jnp.einsum('bhqk,bhkd->bhqd', w.astype(v.dtype), v,
                          preferred_element_type=jnp.float32).astype(v.dtype)

    o = lax.map(block, (qb, jnp.arange(nb)))
    return o.transpose(1, 2, 0, 3, 4).reshape(B, H, S, d)


def forgetting_attention(q, k, v, c):
    B, H, S, d = q.shape
    nb = S // BLOCK_Q
    scale = d ** -0.5
    qb = q.reshape(B, H, nb, BLOCK_Q, d).transpose(2, 0, 1, 3, 4)
    cb = c.reshape(B, H, nb, BLOCK_Q).transpose(2, 0, 1, 3)
    k_pos = jnp.arange(S)

    def block(args):
        qi, ci, i = args
        z = jnp.einsum('bhqd,bhkd->bhqk', qi, k, preferred_element_type=jnp.float32) * scale
        z = z + ci[..., :, None] - c[:, :, None, :]
        q_pos = i * BLOCK_Q + jnp.arange(BLOCK_Q)
        mask = k_pos[None, :] <= q_pos[:, None]
        p = jax.nn.softmax(jnp.where(mask, z, -jnp.inf), axis=-1)
        return jnp.einsum('bhqk,bhkd->bhqd', p.astype(v.dtype), v,
                          preferred_element_type=jnp.float32).astype(v.dtype)

    o = lax.map(block, (qb, cb, jnp.arange(nb)))
    return o.transpose(1, 2, 0, 3, 4).reshape(B, H, S, d)


def hybrid_mixer(x, w_in, b_f, g_sb, g_fox, w_out):
    B, S, _ = x.shape
    proj = jnp.einsum('bsd,de->bse', x, w_in)
    splits = [int(s) for s in np.cumsum([SB_WIDTH] * 3 + [FOX_WIDTH] * 3)]
    q_sb, k_sb, v_sb, q_fx, k_fx, v_fx, f_logit = jnp.split(proj, splits, axis=-1)

    def heads(t, h):
        return t.reshape(B, S, h, HEAD_DIM).transpose(0, 2, 1, 3)

    o_sb = stick_breaking_attention(heads(q_sb, N_SB_HEADS), heads(k_sb, N_SB_HEADS),
                                    heads(v_sb, N_SB_HEADS))
    log_f = jax.nn.log_sigmoid((f_logit + b_f).astype(jnp.float32))
    c = jnp.cumsum(log_f, axis=1).transpose(0, 2, 1)
    o_fx = forgetting_attention(heads(q_fx, N_FOX_HEADS), heads(k_fx, N_FOX_HEADS),
                                heads(v_fx, N_FOX_HEADS), c)
    o = jnp.concatenate([head_rmsnorm(o_sb, g_sb), head_rmsnorm(o_fx, g_fox)], axis=-1)
    return jnp.einsum('bse,ed->bsd', o, w_out)


def swiglu(h, w_gate_up, w_down):
    gu = jnp.einsum('bsd,df->bsf', h, w_gate_up)
    gate, up = jnp.split(gu, 2, axis=-1)
    return jnp.einsum('bsf,fd->bsd', jax.nn.silu(gate) * up, w_down)


def setup_inputs(seed: int = 0) -> dict:
    key = jax.random.key(seed)
    ks = jax.random.split(key, 12)
    f32 = jnp.float32
    x = jax.random.normal(ks[0], (BATCH, SEQ, D_MODEL), f32)
    col_scale = np.ones((IN_WIDTH,), np.float32)
    col_scale[2 * SB_WIDTH:3 * SB_WIDTH] = DEEPNORM_BETA
    col_scale[3 * SB_WIDTH + 2 * FOX_WIDTH:3 * SB_WIDTH + 3 * FOX_WIDTH] = DEEPNORM_BETA
    w_in = (jax.random.normal(ks[1], (DEPTH, D_MODEL, IN_WIDTH), f32)
            * (D_MODEL ** -0.5) * jnp.asarray(col_scale))
    b_f = jax.random.uniform(ks[2], (DEPTH, N_FOX_HEADS), f32, 1.0, 4.0)
    g_sb = 1.0 + 0.02 * jax.random.normal(ks[3], (DEPTH, SB_WIDTH), f32)
    g_fox = 1.0 + 0.02 * jax.random.normal(ks[4], (DEPTH, FOX_WIDTH), f32)
    w_out = jax.random.normal(ks[5], (DEPTH, D_MODEL, D_MODEL), f32) * (D_MODEL ** -0.5) * DEEPNORM_BETA
    ln1_g = 1.0 + 0.02 * jax.random.normal(ks[6], (DEPTH, D_MODEL), f32)
    ln1_b = 0.02 * jax.random.normal(ks[7], (DEPTH, D_MODEL), f32)
    ln2_g = 1.0 + 0.02 * jax.random.normal(ks[8], (DEPTH, D_MODEL), f32)
    ln2_b = 0.02 * jax.random.normal(ks[9], (DEPTH, D_MODEL), f32)
    w_gate_up = jax.random.normal(ks[10], (DEPTH, D_MODEL, 2 * D_FF), f32) * (D_MODEL ** -0.5) * DEEPNORM_BETA
    w_down = jax.random.normal(ks[11], (DEPTH, D_FF, D_MODEL), f32) * (D_FF ** -0.5) * DEEPNORM_BETA
    return {"x": x, "w_in": w_in, "b_f": b_f, "g_sb": g_sb, "g_fox": g_fox, "w_out": w_out,
            "ln1_g": ln1_g, "ln1_b": ln1_b, "ln2_g": ln2_g, "ln2_b": ln2_b,
            "w_gate_up": w_gate_up, "w_down": w_down}


def reference(x, w_in, b_f, g_sb, g_fox, w_out, ln1_g, ln1_b, ln2_g, ln2_b, w_gate_up, w_down):
    h = x
    for l in range(DEPTH):
        mix = hybrid_mixer(h, w_in[l], b_f[l], g_sb[l], g_fox[l], w_out[l])
        h = layer_norm(DEEPNORM_ALPHA * h + mix, ln1_g[l], ln1_b[l])
        ff = swiglu(h, w_gate_up[l], w_down[l])
        h = layer_norm(DEEPNORM_ALPHA * h + ff, ln2_g[l], ln2_b[l])
    return h
```

```python
import functools

import jax
import jax.numpy as jnp
from jax import lax
from jax.experimental import pallas as pl
from jax.experimental.pallas import tpu as pltpu

F32 = jnp.float32
BF16 = jnp.bfloat16

HEAD_DIM = 64
HEAD_PAD = 128
N_SB = 8
N_FX = 8
N_HEADS = N_SB + N_FX
LN_EPS = 1e-5
RMS_EPS = 1e-6

BLOCK_ROWS = 512
BQ = 256
BK = 256
FF_CHUNK = 256
VMEM_LIMIT = 56 * 1024 * 1024

_NT = (((1,), (1,)), ((), ()))
_TN = (((0,), (0,)), ((), ()))


def _const_spec(shape):
    nd = len(shape)
    return pl.BlockSpec(shape, lambda *_: (0,) * nd, pipeline_mode=pl.Buffered(1))


def _proj_kernel(x_ref, wqT_ref, wk_ref, wvT_ref, wf_ref, bf_ref,
                 qT_ref, k_ref, vT_ref, c_ref, carry_ref):
    i = pl.program_id(0)
    bs = x_ref.shape[0]

    @pl.when(i == 0)
    def _():
        carry_ref[...] = jnp.zeros_like(carry_ref)

    xb = x_ref[...]
    xbf = xb.astype(BF16)

    f = jnp.dot(xb, wf_ref[...], precision=lax.Precision.HIGHEST,
                preferred_element_type=F32) + bf_ref[...]
    logf = jnp.minimum(f, 0.0) - jnp.log1p(jnp.exp(-jnp.abs(f)))
    row = lax.broadcasted_iota(jnp.int32, logf.shape, 0)
    c = logf
    d = 1
    while d < bs:
        c = c + jnp.where(row >= d, pltpu.roll(c, d, axis=0), 0.0)
        d *= 2
    c = c + carry_ref[...]
    carry_ref[...] = c[bs - 1:bs, :]
    c_ref[...] = c

    lane = lax.broadcasted_iota(jnp.int32, (bs, HEAD_PAD), 1)
    for hp in range(N_HEADS // 2):
        r = jnp.dot(xbf, wk_ref[:, hp * 256:(hp + 1) * 256],
                    preferred_element_type=F32)
        for sub in range(2):
            h = 2 * hp + sub
            rh = r[:, sub * HEAD_PAD:(sub + 1) * HEAD_PAD]
            if h >= N_SB:
                hh = h - N_SB
                cb = jnp.broadcast_to(c[:, hh:hh + 1], (bs, HEAD_PAD))
                hi = cb.astype(BF16).astype(F32)
                r1 = cb - hi
                mid = r1.astype(BF16).astype(F32)
                lo = (r1 - mid).astype(BF16).astype(F32)
                aug = jnp.where(lane == HEAD_DIM, hi,
                                jnp.where(lane == HEAD_DIM + 1, mid,
                                          jnp.where(lane == HEAD_DIM + 2, lo, 0.0)))
                rh = rh + aug
            k_ref[:, h * HEAD_PAD:(h + 1) * HEAD_PAD] = rh.astype(BF16)

    pad_rows = HEAD_PAD - HEAD_DIM
    prow = lax.broadcasted_iota(jnp.int32, (pad_rows, bs), 0)
    neg_rows = jnp.where(prow < 3, -1.0, 0.0).astype(BF16)
    zero_rows = jnp.zeros((pad_rows, bs), BF16)
    scale = HEAD_DIM ** -0.5
    for cidx in range(4):
        r = lax.dot_general(wqT_ref[cidx * 256:(cidx + 1) * 256, :], xbf, _NT,
                            preferred_element_type=F32)
        for sub in range(4):
            h = cidx * 4 + sub
            base = h * HEAD_PAD
            qT_ref[base:base + HEAD_DIM, :] = (
                r[sub * HEAD_DIM:(sub + 1) * HEAD_DIM, :] * scale).astype(BF16)
            qT_ref[base + HEAD_DIM:base + HEAD_PAD, :] = (
                neg_rows if h >= N_SB else zero_rows)

    for cidx in range(4):
        r = lax.dot_general(wvT_ref[cidx * 256:(cidx + 1) * 256, :], xbf, _NT,
                            preferred_element_type=F32)
        for jb in range(bs // BK):
            vT_ref[jb, cidx * 256:(cidx + 1) * 256, :] = (
                r[:, jb * BK:(jb + 1) * BK].astype(BF16))


def _projection(x2, wqT, wk, wvT, wf, bf):
    s, dm = x2.shape
    bs = BLOCK_ROWS
    return pl.pallas_call(
        _proj_kernel,
        grid=(s // bs,),
        in_specs=[
            pl.BlockSpec((bs, dm), lambda i: (i, 0)),
            _const_spec(wqT.shape),
            _const_spec(wk.shape),
            _const_spec(wvT.shape),
            _const_spec(wf.shape),
            _const_spec(bf.shape),
        ],
        out_specs=[
            pl.BlockSpec((N_HEADS * HEAD_PAD, bs), lambda i: (0, i)),
            pl.BlockSpec((bs, N_HEADS * HEAD_PAD), lambda i: (i, 0)),
            pl.BlockSpec((bs // BK, N_HEADS * HEAD_DIM, BK), lambda i: (i, 0, 0)),
            pl.BlockSpec((bs, N_FX), lambda i: (i, 0)),
        ],
        out_shape=[
            jax.ShapeDtypeStruct((N_HEADS * HEAD_PAD, s), BF16),
            jax.ShapeDtypeStruct((s, N_HEADS * HEAD_PAD), BF16),
            jax.ShapeDtypeStruct((s // BK, N_HEADS * HEAD_DIM, BK), BF16),
            jax.ShapeDtypeStruct((s, N_FX), F32),
        ],
        scratch_shapes=[pltpu.VMEM((1, N_FX), F32)],
        compiler_params=pltpu.CompilerParams(
            dimension_semantics=("arbitrary",), vmem_limit_bytes=VMEM_LIMIT),
        name="proj",
    )(x2, wqT, wk, wvT, wf, bf)


def _head_rmsnorm_T(oT, g_col):
    ms = jnp.mean(oT * oT, axis=0, keepdims=True)
    return oT * lax.rsqrt(ms + RMS_EPS) * g_col


def _diag_iotas():
    srow = lax.broadcasted_iota(jnp.int32, (BK, BQ), 0)
    tcol = lax.broadcasted_iota(jnp.int32, (BK, BQ), 1)
    return srow, tcol


def _sb_kernel(qT_ref, k_ref, vT_ref, tri_ref, g_ref, o_ref):
    i = pl.program_id(1)
    qT = qT_ref[...]

    def block(j, run, acc, masked):
        kb = k_ref[pl.ds(pl.multiple_of(j * BK, BK), BK), :]
        z = jnp.dot(kb, qT, preferred_element_type=F32)
        sp = jnp.maximum(z, 0.0) + jnp.log(1.0 + jnp.exp(-jnp.abs(z)))
        if masked:
            srow, tcol = _diag_iotas()
            valid = srow < tcol
            sp = jnp.where(valid, sp, 0.0)
        hi = sp.astype(BF16)
        lo = (sp - hi.astype(F32)).astype(BF16)
        tri = tri_ref[...]
        csum = (jnp.dot(tri, hi, preferred_element_type=F32)
                + jnp.dot(tri, lo, preferred_element_type=F32))
        w = jnp.exp(z - csum - run)
        if masked:
            w = jnp.where(valid, w, 0.0)
        acc = acc + jnp.dot(vT_ref[j], w.astype(BF16), preferred_element_type=F32)
        return run + csum[0:1, :], acc

    run0 = jnp.zeros((1, BQ), F32)
    acc0 = jnp.zeros((HEAD_DIM, BQ), F32)
    run, acc = block(i, run0, acc0, True)

    def body(jj, carry):
        return block(i - 1 - jj, carry[0], carry[1], False)

    run, acc = lax.fori_loop(0, i, body, (run, acc))
    o_ref[...] = _head_rmsnorm_T(acc, g_ref[...]).astype(o_ref.dtype)


def _fx_kernel(qT_ref, k_ref, vT_ref, ct_ref, g_ref, o_ref):
    i = pl.program_id(1)
    qT = qT_ref[...]
    ct = ct_ref[...]

    def block(j, m, l, acc, masked):
        kb = k_ref[pl.ds(pl.multiple_of(j * BK, BK), BK), :]
        u = jnp.dot(kb, qT, preferred_element_type=F32)
        if masked:
            srow, tcol = _diag_iotas()
            u = jnp.where(srow <= tcol, u, -jnp.inf)
        m_new = jnp.maximum(m, jnp.max(u, axis=0, keepdims=True) + ct)
        p = jnp.exp(u + (ct - m_new))
        alpha = jnp.exp(m - m_new)
        l = alpha * l + jnp.sum(p, axis=0, keepdims=True)
        acc = alpha * acc + jnp.dot(vT_ref[j], p.astype(BF16),
                                    preferred_element_type=F32)
        return m_new, l, acc

    m0 = jnp.full((1, BQ), -jnp.inf, F32)
    l0 = jnp.zeros((1, BQ), F32)
    acc0 = jnp.zeros((HEAD_DIM, BQ), F32)
    m, l, acc = block(i, m0, l0, acc0, True)

    def body(jj, carry):
        return block(i - 1 - jj, carry[0], carry[1], carry[2], False)

    m, l, acc = lax.fori_loop(0, i, body, (m, l, acc))
    o_ref[...] = _head_rmsnorm_T(acc / l, g_ref[...]).astype(o_ref.dtype)


def _attention(kernel_fn, head0, qT, k, vT, extra, extra_spec, g_col, name):
    s = k.shape[0]
    nq = s // BQ
    nkb = s // BK
    return pl.pallas_call(
        kernel_fn,
        grid=(N_SB, nq),
        in_specs=[
            pl.BlockSpec((HEAD_PAD, BQ), lambda h, i: (head0 + h, i)),
            pl.BlockSpec((s, HEAD_PAD), lambda h, i: (0, head0 + h)),
            pl.BlockSpec((nkb, HEAD_DIM, BK), lambda h, i: (0, head0 + h, 0)),
            extra_spec,
            pl.BlockSpec((HEAD_DIM, 1), lambda h, i: (h, 0)),
        ],
        out_specs=pl.BlockSpec((HEAD_DIM, BQ), lambda h, i: (h, i)),
        out_shape=jax.ShapeDtypeStruct((N_SB * HEAD_DIM, s), BF16),
        compiler_params=pltpu.CompilerParams(
            dimension_semantics=("arbitrary", "arbitrary"),
            vmem_limit_bytes=VMEM_LIMIT),
        name=name,
    )(qT, k, vT, extra, g_col)


def _layer_norm(v, g, b):
    mu = jnp.mean(v, axis=-1, keepdims=True)
    vc = v - mu
    var = jnp.mean(vc * vc, axis=-1, keepdims=True)
    return vc * lax.rsqrt(var + LN_EPS) * g + b


def _ffn_kernel(alpha, x_ref, osb_ref, ofx_ref, woa_ref, wob_ref,
                g1_ref, b1_ref, g2_ref, b2_ref, wgu_ref, wd_ref,
                y_ref, act_ref):
    d_ff = wd_ref.shape[0]
    mix = (lax.dot_general(osb_ref[...], woa_ref[...], _TN, preferred_element_type=F32)
           + lax.dot_general(ofx_ref[...], wob_ref[...], _TN, preferred_element_type=F32))
    h1 = _layer_norm(alpha * x_ref[...] + mix, g1_ref[...], b1_ref[...])
    h1b = h1.astype(BF16)
    for c in range(d_ff // FF_CHUNK):
        lo, hi = c * FF_CHUNK, (c + 1) * FF_CHUNK
        gate = jnp.dot(h1b, wgu_ref[:, lo:hi], preferred_element_type=F32)
        up = jnp.dot(h1b, wgu_ref[:, d_ff + lo:d_ff + hi], preferred_element_type=F32)
        act_ref[:, lo:hi] = (gate * jax.nn.sigmoid(gate) * up).astype(BF16)
    ff = jnp.dot(act_ref[...], wd_ref[...], preferred_element_type=F32)
    y_ref[...] = _layer_norm(alpha * h1 + ff, g2_ref[...], b2_ref[...])


def _out_ffn(alpha, x2, oT_sb, oT_fx, wo_a, wo_b, g1, b1, g2, b2, wgu, wd):
    s, dm = x2.shape
    bs = BLOCK_ROWS
    d_ff = wd.shape[0]
    half = oT_sb.shape[0]
    return pl.pallas_call(
        functools.partial(_ffn_kernel, alpha),
        grid=(s // bs,),
        in_specs=[
            pl.BlockSpec((bs, dm), lambda i: (i, 0)),
            pl.BlockSpec((half, bs), lambda i: (0, i)),
            pl.BlockSpec((half, bs), lambda i: (0, i)),
            _const_spec(wo_a.shape),
            _const_spec(wo_b.shape),
            _const_spec(g1.shape),
            _const_spec(b1.shape),
            _const_spec(g2.shape),
            _const_spec(b2.shape),
            _const_spec(wgu.shape),
            _const_spec(wd.shape),
        ],
        out_specs=pl.BlockSpec((bs, dm), lambda i: (i, 0)),
        out_shape=jax.ShapeDtypeStruct((s, dm), F32),
        scratch_shapes=[pltpu.VMEM((bs, d_ff), BF16)],
        compiler_params=pltpu.CompilerParams(
            dimension_semantics=("arbitrary",), vmem_limit_bytes=VMEM_LIMIT),
        name="out_ffn",
    )(x2, oT_sb, oT_fx, wo_a, wo_b, g1, b1, g2, b2, wgu, wd)


def _layer(x2, w_in, b_f, g_sb, g_fox, w_out, ln1_g, ln1_b, ln2_g, ln2_b,
           w_gate_up, w_down, alpha):
    s, dm = x2.shape
    sbw = N_SB * HEAD_DIM
    fxw = N_FX * HEAD_DIM
    q_sb, k_sb, v_sb = w_in[:, :sbw], w_in[:, sbw:2 * sbw], w_in[:, 2 * sbw:3 * sbw]
    o = 3 * sbw
    q_fx, k_fx, v_fx = w_in[:, o:o + fxw], w_in[:, o + fxw:o + 2 * fxw], w_in[:, o + 2 * fxw:o + 3 * fxw]
    wf = w_in[:, o + 3 * fxw:]

    wqT = jnp.concatenate([q_sb, q_fx], axis=1).T.astype(BF16)
    wk = jnp.concatenate([k_sb, k_fx], axis=1).reshape(dm, N_HEADS, HEAD_DIM)
    wk = jnp.pad(wk, ((0, 0), (0, 0), (0, HEAD_PAD - HEAD_DIM)))
    wk = wk.reshape(dm, N_HEADS * HEAD_PAD).astype(BF16)
    wvT = jnp.concatenate([v_sb, v_fx], axis=1).T.astype(BF16)

    qT, k, vT, c = _projection(x2, wqT, wk, wvT, wf, b_f.reshape(1, N_FX))

    srow = lax.broadcasted_iota(jnp.int32, (BK, BK), 0)
    jcol = lax.broadcasted_iota(jnp.int32, (BK, BK), 1)
    tri = (jcol >= srow).astype(BF16)
    ct = c.T.reshape(N_FX, 1, s)

    oT_sb = _attention(
        _sb_kernel, 0, qT, k, vT, tri,
        pl.BlockSpec((BK, BK), lambda h, i: (0, 0)),
        g_sb.reshape(sbw, 1), "sb_attn")
    oT_fx = _attention(
        _fx_kernel, N_SB, qT, k, vT, ct,
        pl.BlockSpec((None, 1, BQ), lambda h, i: (h, 0, i)),
        g_fox.reshape(fxw, 1), "fx_attn")

    wo = w_out.astype(BF16)
    return _out_ffn(
        alpha, x2, oT_sb, oT_fx, wo[:sbw], wo[sbw:],
        ln1_g.reshape(1, dm), ln1_b.reshape(1, dm),
        ln2_g.reshape(1, dm), ln2_b.reshape(1, dm),
        w_gate_up.astype(BF16), w_down.astype(BF16))


def kernel(x, w_in, b_f, g_sb, g_fox, w_out, ln1_g, ln1_b, ln2_g, ln2_b, w_gate_up, w_down):
    batch, s, dm = x.shape
    depth = w_in.shape[0]
    alpha = (2 * depth) ** 0.25
    outs = []
    for b in range(batch):
        h = x[b]
        for l in range(depth):
            h = _layer(h, w_in[l], b_f[l], g_sb[l], g_fox[l], w_out[l],
                       ln1_g[l], ln1_b[l], ln2_g[l], ln2_b[l],
                       w_gate_up[l], w_down[l], alpha)
        outs.append(h)
    return outs[0][None] if batch == 1 else jnp.stack(outs, axis=0)
```

```python
import functools

import jax
import jax.numpy as jnp
from jax import lax
from jax.experimental import pallas as pl
from jax.experimental.pallas import tpu as pltpu

F32 = jnp.float32
BF16 = jnp.bfloat16

HEAD_DIM = 64
HEAD_PAD = 128
N_SB = 8
N_FX = 8
N_HEADS = N_SB + N_FX
LN_EPS = 1e-5
RMS_EPS = 1e-6

BLOCK_ROWS = 512
BQ = 512
BK = 256
NSUB = BQ // BK
FF_CHUNK = 256
VMEM_LIMIT = 56 * 1024 * 1024

_NT = (((1,), (1,)), ((), ()))
_TN = (((0,), (0,)), ((), ()))


def _const_spec(shape):
    nd = len(shape)
    return pl.BlockSpec(shape, lambda *_: (0,) * nd, pipeline_mode=pl.Buffered(1))


def _proj_kernel(x_ref, wqT_ref, wk_ref, wvT_ref, wf_ref, bf_ref,
                 qT_ref, k_ref, vT_ref, c_ref, carry_ref):
    i = pl.program_id(0)
    bs = x_ref.shape[0]

    @pl.when(i == 0)
    def _():
        carry_ref[...] = jnp.zeros_like(carry_ref)

    xb = x_ref[...]
    xbf = xb.astype(BF16)

    f = jnp.dot(xb, wf_ref[...], precision=lax.Precision.HIGHEST,
                preferred_element_type=F32) + bf_ref[...]
    logf = jnp.minimum(f, 0.0) - jnp.log1p(jnp.exp(-jnp.abs(f)))
    row = lax.broadcasted_iota(jnp.int32, logf.shape, 0)
    c = logf
    d = 1
    while d < bs:
        c = c + jnp.where(row >= d, pltpu.roll(c, d, axis=0), 0.0)
        d *= 2
    c = c + carry_ref[...]
    carry_ref[...] = c[bs - 1:bs, :]
    c_ref[...] = c

    lane = lax.broadcasted_iota(jnp.int32, (bs, HEAD_PAD), 1)
    for hp in range(N_HEADS // 2):
        r = jnp.dot(xbf, wk_ref[:, hp * 256:(hp + 1) * 256],
                    preferred_element_type=F32)
        for sub in range(2):
            h = 2 * hp + sub
            rh = r[:, sub * HEAD_PAD:(sub + 1) * HEAD_PAD]
            if h >= N_SB:
                hh = h - N_SB
                cb = jnp.broadcast_to(c[:, hh:hh + 1], (bs, HEAD_PAD))
                hi = cb.astype(BF16).astype(F32)
                r1 = cb - hi
                mid = r1.astype(BF16).astype(F32)
                lo = (r1 - mid).astype(BF16).astype(F32)
                aug = jnp.where(lane == HEAD_DIM, hi,
                                jnp.where(lane == HEAD_DIM + 1, mid,
                                          jnp.where(lane == HEAD_DIM + 2, lo, 0.0)))
                rh = rh + aug
            k_ref[:, h * HEAD_PAD:(h + 1) * HEAD_PAD] = rh.astype(BF16)

    pad_rows = HEAD_PAD - HEAD_DIM
    prow = lax.broadcasted_iota(jnp.int32, (pad_rows, bs), 0)
    neg_rows = jnp.where(prow < 3, -1.0, 0.0).astype(BF16)
    zero_rows = jnp.zeros((pad_rows, bs), BF16)
    scale = HEAD_DIM ** -0.5
    for cidx in range(4):
        r = lax.dot_general(wqT_ref[cidx * 256:(cidx + 1) * 256, :], xbf, _NT,
                            preferred_element_type=F32)
        for sub in range(4):
            h = cidx * 4 + sub
            base = h * HEAD_PAD
            qT_ref[base:base + HEAD_DIM, :] = (
                r[sub * HEAD_DIM:(sub + 1) * HEAD_DIM, :] * scale).astype(BF16)
            qT_ref[base + HEAD_DIM:base + HEAD_PAD, :] = (
                neg_rows if h >= N_SB else zero_rows)

    for cidx in range(4):
        r = lax.dot_general(wvT_ref[cidx * 256:(cidx + 1) * 256, :], xbf, _NT,
                            preferred_element_type=F32)
        for jb in range(bs // BK):
            vT_ref[jb, cidx * 256:(cidx + 1) * 256, :] = (
                r[:, jb * BK:(jb + 1) * BK].astype(BF16))


def _projection(x2, wqT, wk, wvT, wf, bf):
    s, dm = x2.shape
    bs = BLOCK_ROWS
    return pl.pallas_call(
        _proj_kernel,
        grid=(s // bs,),
        in_specs=[
            pl.BlockSpec((bs, dm), lambda i: (i, 0)),
            _const_spec(wqT.shape),
            _const_spec(wk.shape),
            _const_spec(wvT.shape),
            _const_spec(wf.shape),
            _const_spec(bf.shape),
        ],
        out_specs=[
            pl.BlockSpec((N_HEADS * HEAD_PAD, bs), lambda i: (0, i)),
            pl.BlockSpec((bs, N_HEADS * HEAD_PAD), lambda i: (i, 0)),
            pl.BlockSpec((bs // BK, N_HEADS * HEAD_DIM, BK), lambda i: (i, 0, 0)),
            pl.BlockSpec((bs, N_FX), lambda i: (i, 0)),
        ],
        out_shape=[
            jax.ShapeDtypeStruct((N_HEADS * HEAD_PAD, s), BF16),
            jax.ShapeDtypeStruct((s, N_HEADS * HEAD_PAD), BF16),
            jax.ShapeDtypeStruct((s // BK, N_HEADS * HEAD_DIM, BK), BF16),
            jax.ShapeDtypeStruct((s, N_FX), F32),
        ],
        scratch_shapes=[pltpu.VMEM((1, N_FX), F32)],
        compiler_params=pltpu.CompilerParams(
            dimension_semantics=("arbitrary",), vmem_limit_bytes=VMEM_LIMIT),
        name="proj",
    )(x2, wqT, wk, wvT, wf, bf)


def _head_rmsnorm_T(oT, g_col):
    ms = jnp.mean(oT * oT, axis=0, keepdims=True)
    return oT * lax.rsqrt(ms + RMS_EPS) * g_col


def _diag_iotas(sub):
    srow = lax.broadcasted_iota(jnp.int32, (BK, BQ), 0) + sub * BK
    tcol = lax.broadcasted_iota(jnp.int32, (BK, BQ), 1)
    return srow, tcol


def _sb_kernel(qT_ref, k_ref, vT_ref, tri_ref, g_ref, o_ref,
               z0_ref, hi0_ref, lo0_ref, w0_ref, z1_ref, hi1_ref, lo1_ref, w1_ref,
               acc_ref, run_ref):
    i = pl.program_id(1)
    sets = ((z0_ref, hi0_ref, lo0_ref, w0_ref), (z1_ref, hi1_ref, lo1_ref, w1_ref))
    n_tiles = NSUB * (i + 1)

    def key_tile(t):
        return n_tiles - 1 - t

    def step(score_t=None, csum_t=None, pv_t=None, score_diag=None, csum_diag=None):
        if score_t is not None:
            j = key_tile(score_t[0])
            kb = k_ref[pl.ds(pl.multiple_of(j * BK, BK), BK), :]
            z = jnp.dot(kb, qT_ref[...], preferred_element_type=F32)
        if csum_t is not None:
            z_ref, hi_ref, lo_ref, w_ref = sets[csum_t[1]]
            tri = tri_ref[...]
            csum = (jnp.dot(tri, hi_ref[...], preferred_element_type=F32)
                    + jnp.dot(tri, lo_ref[...], preferred_element_type=F32))
        if pv_t is not None:
            acc_ref[...] += jnp.dot(vT_ref[key_tile(pv_t[0])], sets[pv_t[1]][3][...],
                                    preferred_element_type=F32)
        if score_t is not None:
            zs_ref, his_ref, los_ref, _ = sets[score_t[1]]
            sp = jnp.maximum(z, 0.0) + jnp.log(1.0 + jnp.exp(-jnp.abs(z)))
            if score_diag is not None:
                srow, tcol = _diag_iotas(score_diag)
                sp = jnp.where(srow < tcol, sp, 0.0)
            hi = sp.astype(BF16)
            zs_ref[...] = z
            his_ref[...] = hi
            los_ref[...] = (sp - hi.astype(F32)).astype(BF16)
        if csum_t is not None:
            w = jnp.exp(z_ref[...] - csum - run_ref[...])
            if csum_diag is not None:
                srow, tcol = _diag_iotas(csum_diag)
                w = jnp.where(srow < tcol, w, 0.0)
            w_ref[...] = w.astype(BF16)
            run_ref[...] += csum[0:1, :]

    acc_ref[...] = jnp.zeros_like(acc_ref)
    run_ref[...] = jnp.zeros_like(run_ref)

    step(score_t=(0, 0), score_diag=1)
    step(score_t=(1, 1), score_diag=0, csum_t=(0, 0), csum_diag=1)

    @pl.when(i == 0)
    def _():
        step(csum_t=(1, 1), csum_diag=0, pv_t=(0, 0))
        step(pv_t=(1, 1))

    @pl.when(i > 0)
    def _():
        step(score_t=(2, 0), csum_t=(1, 1), csum_diag=0, pv_t=(0, 0))
        step(score_t=(3, 1), csum_t=(2, 0), pv_t=(1, 1))

        def body(jj, carry):
            t = 4 + 2 * jj
            step(score_t=(t, 0), csum_t=(t - 1, 1), pv_t=(t - 2, 0))
            step(score_t=(t + 1, 1), csum_t=(t, 0), pv_t=(t - 1, 1))
            return carry

        lax.fori_loop(0, i - 1, body, 0)
        step(csum_t=(n_tiles - 1, 1), pv_t=(n_tiles - 2, 0))
        step(pv_t=(n_tiles - 1, 1))

    o_ref[...] = _head_rmsnorm_T(acc_ref[...], g_ref[...]).astype(o_ref.dtype)


def _fx_kernel(qT_ref, k_ref, vT_ref, ct_ref, g_ref, o_ref,
               ua_ref, ma_ref, ub_ref, mb_ref, acc_ref, m_ref, l_ref):
    i = pl.program_id(1)
    bufs_a = (ua_ref, ma_ref)
    bufs_b = (ub_ref, mb_ref)

    def score(j, bufs, diag_sub=None):
        u_ref, mx_ref = bufs
        kb = k_ref[pl.ds(pl.multiple_of(j * BK, BK), BK), :]
        u = jnp.dot(kb, qT_ref[...], preferred_element_type=F32)
        if diag_sub is not None:
            srow, tcol = _diag_iotas(diag_sub)
            u = jnp.where(srow <= tcol, u, -jnp.inf)
        u_ref[...] = u
        mx_ref[...] = jnp.max(u, axis=0, keepdims=True)

    def apply(j, bufs):
        u_ref, mx_ref = bufs
        ct = ct_ref[...]
        m_old = m_ref[...]
        m_new = jnp.maximum(m_old, mx_ref[...] + ct)
        p = jnp.exp(u_ref[...] + (ct - m_new))
        alpha = jnp.exp(m_old - m_new)
        m_ref[...] = m_new
        l_ref[...] = alpha * l_ref[...] + jnp.sum(p, axis=0, keepdims=True)
        acc_ref[...] = alpha * acc_ref[...] + jnp.dot(
            vT_ref[j], p.astype(BF16), preferred_element_type=F32)

    acc_ref[...] = jnp.zeros_like(acc_ref)
    l_ref[...] = jnp.zeros_like(l_ref)
    m_ref[...] = jnp.full(m_ref.shape, -jnp.inf, F32)

    for sub in range(NSUB):
        score(i * NSUB + sub, bufs_a, sub)
        apply(i * NSUB + sub, bufs_a)

    score(jnp.maximum(i - 1, 0) * NSUB + 1, bufs_a)

    def body(jj, carry):
        jsup = i - 1 - jj
        apply(jsup * NSUB + 1, bufs_a)
        score(jsup * NSUB, bufs_b)
        apply(jsup * NSUB, bufs_b)
        score(jnp.maximum(jsup - 1, 0) * NSUB + 1, bufs_a)
        return carry

    lax.fori_loop(0, i, body, 0)
    o_ref[...] = _head_rmsnorm_T(acc_ref[...] / l_ref[...], g_ref[...]).astype(o_ref.dtype)


def _sb_scratch():
    tile_f32 = pltpu.VMEM((BK, BQ), F32)
    tile_bf16 = pltpu.VMEM((BK, BQ), BF16)
    one_set = [tile_f32, tile_bf16, tile_bf16, tile_bf16]
    return one_set + one_set + [pltpu.VMEM((HEAD_DIM, BQ), F32), pltpu.VMEM((1, BQ), F32)]


def _fx_scratch():
    tile_f32 = pltpu.VMEM((BK, BQ), F32)
    row = pltpu.VMEM((1, BQ), F32)
    return [tile_f32, row, tile_f32, row,
            pltpu.VMEM((HEAD_DIM, BQ), F32), row, row]


def _attention(kernel_fn, head0, qT, k, vT, extra, extra_spec, g_col, scratch, name):
    s = k.shape[0]
    nq = s // BQ
    nkb = s // BK
    return pl.pallas_call(
        kernel_fn,
        grid=(N_SB, nq),
        in_specs=[
            pl.BlockSpec((HEAD_PAD, BQ), lambda h, i: (head0 + h, i)),
            pl.BlockSpec((s, HEAD_PAD), lambda h, i: (0, head0 + h)),
            pl.BlockSpec((nkb, HEAD_DIM, BK), lambda h, i: (0, head0 + h, 0)),
            extra_spec,
            pl.BlockSpec((HEAD_DIM, 1), lambda h, i: (h, 0)),
        ],
        out_specs=pl.BlockSpec((HEAD_DIM, BQ), lambda h, i: (h, i)),
        out_shape=jax.ShapeDtypeStruct((N_SB * HEAD_DIM, s), BF16),
        scratch_shapes=scratch,
        compiler_params=pltpu.CompilerParams(
            dimension_semantics=("arbitrary", "arbitrary"),
            vmem_limit_bytes=VMEM_LIMIT),
        name=name,
    )(qT, k, vT, extra, g_col)


def _layer_norm(v, g, b):
    mu = jnp.mean(v, axis=-1, keepdims=True)
    vc = v - mu
    var = jnp.mean(vc * vc, axis=-1, keepdims=True)
    return vc * lax.rsqrt(var + LN_EPS) * g + b


def _ffn_kernel(alpha, x_ref, osb_ref, ofx_ref, woa_ref, wob_ref,
                g1_ref, b1_ref, g2_ref, b2_ref, wgu_ref, wd_ref,
                y_ref, act_ref):
    d_ff = wd_ref.shape[0]
    mix = (lax.dot_general(osb_ref[...], woa_ref[...], _TN, preferred_element_type=F32)
           + lax.dot_general(ofx_ref[...], wob_ref[...], _TN, preferred_element_type=F32))
    h1 = _layer_norm(alpha * x_ref[...] + mix, g1_ref[...], b1_ref[...])
    h1b = h1.astype(BF16)
    for c in range(d_ff // FF_CHUNK):
        lo, hi = c * FF_CHUNK, (c + 1) * FF_CHUNK
        gate = jnp.dot(h1b, wgu_ref[:, lo:hi], preferred_element_type=F32)
        up = jnp.dot(h1b, wgu_ref[:, d_ff + lo:d_ff + hi], preferred_element_type=F32)
        act_ref[:, lo:hi] = (gate * jax.nn.sigmoid(gate) * up).astype(BF16)
    ff = jnp.dot(act_ref[...], wd_ref[...], preferred_element_type=F32)
    y_ref[...] = _layer_norm(alpha * h1 + ff, g2_ref[...], b2_ref[...])


def _out_ffn(alpha, x2, oT_sb, oT_fx, wo_a, wo_b, g1, b1, g2, b2, wgu, wd):
    s, dm = x2.shape
    bs = BLOCK_ROWS
    d_ff = wd.shape[0]
    half = oT_sb.shape[0]
    return pl.pallas_call(
        functools.partial(_ffn_kernel, alpha),
        grid=(s // bs,),
        in_specs=[
            pl.BlockSpec((bs, dm), lambda i: (i, 0)),
            pl.BlockSpec((half, bs), lambda i: (0, i)),
            pl.BlockSpec((half, bs), lambda i: (0, i)),
            _const_spec(wo_a.shape),
            _const_spec(wo_b.shape),
            _const_spec(g1.shape),
            _const_spec(b1.shape),
            _const_spec(g2.shape),
            _const_spec(b2.shape),
            _const_spec(wgu.shape),
            _const_spec(wd.shape),
        ],
        out_specs=pl.BlockSpec((bs, dm), lambda i: (i, 0)),
        out_shape=jax.ShapeDtypeStruct((s, dm), F32),
        scratch_shapes=[pltpu.VMEM((bs, d_ff), BF16)],
        compiler_params=pltpu.CompilerParams(
            dimension_semantics=("arbitrary",), vmem_limit_bytes=VMEM_LIMIT),
        name="out_ffn",
    )(x2, oT_sb, oT_fx, wo_a, wo_b, g1, b1, g2, b2, wgu, wd)


def _layer(x2, w_in, b_f, g_sb, g_fox, w_out, ln1_g, ln1_b, ln2_g, ln2_b,
           w_gate_up, w_down, alpha):
    s, dm = x2.shape
    sbw = N_SB * HEAD_DIM
    fxw = N_FX * HEAD_DIM
    q_sb, k_sb, v_sb = w_in[:, :sbw], w_in[:, sbw:2 * sbw], w_in[:, 2 * sbw:3 * sbw]
    o = 3 * sbw
    q_fx, k_fx, v_fx = w_in[:, o:o + fxw], w_in[:, o + fxw:o + 2 * fxw], w_in[:, o + 2 * fxw:o + 3 * fxw]
    wf = w_in[:, o + 3 * fxw:]

    wqT = jnp.concatenate([q_sb, q_fx], axis=1).T.astype(BF16)
    wk = jnp.concatenate([k_sb, k_fx], axis=1).reshape(dm, N_HEADS, HEAD_DIM)
    wk = jnp.pad(wk, ((0, 0), (0, 0), (0, HEAD_PAD - HEAD_DIM)))
    wk = wk.reshape(dm, N_HEADS * HEAD_PAD).astype(BF16)
    wvT = jnp.concatenate([v_sb, v_fx], axis=1).T.astype(BF16)

    qT, k, vT, c = _projection(x2, wqT, wk, wvT, wf, b_f.reshape(1, N_FX))

    srow = lax.broadcasted_iota(jnp.int32, (BK, BK), 0)
    jcol = lax.broadcasted_iota(jnp.int32, (BK, BK), 1)
    tri = (jcol >= srow).astype(BF16)
    ct = c.T.reshape(N_FX, 1, s)

    oT_sb = _attention(
        _sb_kernel, 0, qT, k, vT, tri,
        pl.BlockSpec((BK, BK), lambda h, i: (0, 0)),
        g_sb.reshape(sbw, 1), _sb_scratch(), "sb_attn")
    oT_fx = _attention(
        _fx_kernel, N_SB, qT, k, vT, ct,
        pl.BlockSpec((None, 1, BQ), lambda h, i: (h, 0, i)),
        g_fox.reshape(fxw, 1), _fx_scratch(), "fx_attn")

    wo = w_out.astype(BF16)
    return _out_ffn(
        alpha, x2, oT_sb, oT_fx, wo[:sbw], wo[sbw:],
        ln1_g.reshape(1, dm), ln1_b.reshape(1, dm),
        ln2_g.reshape(1, dm), ln2_b.reshape(1, dm),
        w_gate_up.astype(BF16), w_down.astype(BF16))


def kernel(x, w_in, b_f, g_sb, g_fox, w_out, ln1_g, ln1_b, ln2_g, ln2_b, w_gate_up, w_down):
    batch, s, dm = x.shape
    depth = w_in.shape[0]
    alpha = (2 * depth) ** 0.25
    outs = []
    for b in range(batch):
        h = x[b]
        for l in range(depth):
            h = _layer(h, w_in[l], b_f[l], g_sb[l], g_fox[l], w_out[l],
                       ln1_g[l], ln1_b[l], ln2_g[l], ln2_b[l],
                       w_gate_up[l], w_down[l], alpha)
        outs.append(h)
    return outs[0][None] if batch == 1 else jnp.stack(outs, axis=0)
```

```python
import functools

import jax
import jax.numpy as jnp
from jax import lax
from jax.experimental import pallas as pl
from jax.experimental.pallas import tpu as pltpu

F32 = jnp.float32
BF16 = jnp.bfloat16

HEAD_DIM = 64
HEAD_PAD = 128
N_SB = 8
N_FX = 8
N_HEADS = N_SB + N_FX
LN_EPS = 1e-5
RMS_EPS = 1e-6
LOG2E = 1.4426950408889634

BLOCK_ROWS = 512
BQ = 512
BK = 256
NSUB = BQ // BK
FF_CHUNK = 256
VMEM_LIMIT = 56 * 1024 * 1024

SB_DEAD_LOG2 = 160.0
FX_DEAD_LN = 110.0
FX_NORM_SLACK = 1.001

_NT = (((1,), (1,)), ((), ()))
_TN = (((0,), (0,)), ((), ()))


def _const_spec(shape):
    nd = len(shape)
    return pl.BlockSpec(shape, lambda *_: (0,) * nd, pipeline_mode=pl.Buffered(1))


def _proj_kernel(x_ref, wqT_ref, wk_ref, wvT_ref, wf_ref, bf_ref,
                 qT_ref, k_ref, vT_ref, c_ref, kpre_ref, cend_ref,
                 carry_ref, kmax_ref):
    i = pl.program_id(0)
    bs = x_ref.shape[0]
    n_kt = bs // BK

    @pl.when(i == 0)
    def _():
        carry_ref[...] = jnp.zeros_like(carry_ref)
        kmax_ref[...] = jnp.zeros_like(kmax_ref)

    xb = x_ref[...]
    xbf = xb.astype(BF16)

    f = jnp.dot(xb, wf_ref[...], precision=lax.Precision.HIGHEST,
                preferred_element_type=F32) + bf_ref[...]
    logf = jnp.minimum(f, 0.0) - jnp.log1p(jnp.exp(-jnp.abs(f)))
    row = lax.broadcasted_iota(jnp.int32, logf.shape, 0)
    c = logf
    d = 1
    while d < bs:
        c = c + jnp.where(row >= d, pltpu.roll(c, d, axis=0), 0.0)
        d *= 2
    c = c + carry_ref[...]
    carry_ref[...] = c[bs - 1:bs, :]
    c_ref[...] = c
    stat_tile = lax.broadcasted_iota(jnp.int32, (n_kt, N_FX), 0)
    stat_head = lax.broadcasted_iota(jnp.int32, (n_kt, N_FX), 1)
    c_end = jnp.zeros((n_kt, N_FX), F32)
    for t in range(n_kt):
        last = jnp.broadcast_to(c[(t + 1) * BK - 1:(t + 1) * BK, :], (n_kt, N_FX))
        c_end = jnp.where(stat_tile == t, last, c_end)
    cend_ref[0] = c_end

    lane = lax.broadcasted_iota(jnp.int32, (bs, HEAD_PAD), 1)
    knorm2 = jnp.zeros((n_kt, N_FX), F32)
    for hp in range(N_HEADS // 2):
        r = jnp.dot(xbf, wk_ref[:, hp * 256:(hp + 1) * 256],
                    preferred_element_type=F32)
        for sub in range(2):
            h = 2 * hp + sub
            rh = r[:, sub * HEAD_PAD:(sub + 1) * HEAD_PAD]
            if h >= N_SB:
                hh = h - N_SB
                kr = rh.astype(BF16).astype(F32)
                n2 = jnp.sum(kr * kr, axis=1, keepdims=True)
                for t in range(n_kt):
                    tmax = jnp.max(n2[t * BK:(t + 1) * BK, :], axis=0, keepdims=True)
                    knorm2 = jnp.where((stat_tile == t) & (stat_head == hh),
                                       jnp.broadcast_to(tmax, (n_kt, N_FX)), knorm2)
                cb = jnp.broadcast_to(c[:, hh:hh + 1], (bs, HEAD_PAD))
                hi = cb.astype(BF16).astype(F32)
                r1 = cb - hi
                mid = r1.astype(BF16).astype(F32)
                lo = (r1 - mid).astype(BF16).astype(F32)
                aug = jnp.where(lane == HEAD_DIM, hi,
                                jnp.where(lane == HEAD_DIM + 1, mid,
                                          jnp.where(lane == HEAD_DIM + 2, lo, 0.0)))
                rh = rh + aug
            k_ref[:, h * HEAD_PAD:(h + 1) * HEAD_PAD] = rh.astype(BF16)

    knorm = jnp.sqrt(knorm2)
    k_pre = jnp.zeros((n_kt, N_FX), F32)
    run_max = kmax_ref[...]
    for t in range(n_kt):
        run_max = jnp.maximum(run_max, knorm[t:t + 1, :])
        k_pre = jnp.where(stat_tile == t, jnp.broadcast_to(run_max, (n_kt, N_FX)), k_pre)
    kmax_ref[...] = run_max
    kpre_ref[0] = k_pre

    pad_rows = HEAD_PAD - HEAD_DIM
    prow = lax.broadcasted_iota(jnp.int32, (pad_rows, bs), 0)
    neg_rows = jnp.where(prow < 3, -1.0, 0.0).astype(BF16)
    zero_rows = jnp.zeros((pad_rows, bs), BF16)
    for cidx in range(4):
        r = lax.dot_general(wqT_ref[cidx * 256:(cidx + 1) * 256, :], xbf, _NT,
                            preferred_element_type=F32)
        for sub in range(4):
            h = cidx * 4 + sub
            base = h * HEAD_PAD
            scale = HEAD_DIM ** -0.5 * (LOG2E if h < N_SB else 1.0)
            qT_ref[base:base + HEAD_DIM, :] = (
                r[sub * HEAD_DIM:(sub + 1) * HEAD_DIM, :] * scale).astype(BF16)
            qT_ref[base + HEAD_DIM:base + HEAD_PAD, :] = (
                neg_rows if h >= N_SB else zero_rows)

    for cidx in range(4):
        r = lax.dot_general(wvT_ref[cidx * 256:(cidx + 1) * 256, :], xbf, _NT,
                            preferred_element_type=F32)
        for jb in range(n_kt):
            vT_ref[jb, cidx * 256:(cidx + 1) * 256, :] = (
                r[:, jb * BK:(jb + 1) * BK].astype(BF16))


def _projection(x2, wqT, wk, wvT, wf, bf):
    s, dm = x2.shape
    bs = BLOCK_ROWS
    n_kt = bs // BK
    return pl.pallas_call(
        _proj_kernel,
        grid=(s // bs,),
        in_specs=[
            pl.BlockSpec((bs, dm), lambda i: (i, 0)),
            _const_spec(wqT.shape),
            _const_spec(wk.shape),
            _const_spec(wvT.shape),
            _const_spec(wf.shape),
            _const_spec(bf.shape),
        ],
        out_specs=[
            pl.BlockSpec((N_HEADS * HEAD_PAD, bs), lambda i: (0, i)),
            pl.BlockSpec((bs, N_HEADS * HEAD_PAD), lambda i: (i, 0)),
            pl.BlockSpec((n_kt, N_HEADS * HEAD_DIM, BK), lambda i: (i, 0, 0)),
            pl.BlockSpec((bs, N_FX), lambda i: (i, 0)),
            pl.BlockSpec((1, n_kt, N_FX), lambda i: (i, 0, 0)),
            pl.BlockSpec((1, n_kt, N_FX), lambda i: (i, 0, 0)),
        ],
        out_shape=[
            jax.ShapeDtypeStruct((N_HEADS * HEAD_PAD, s), BF16),
            jax.ShapeDtypeStruct((s, N_HEADS * HEAD_PAD), BF16),
            jax.ShapeDtypeStruct((s // BK, N_HEADS * HEAD_DIM, BK), BF16),
            jax.ShapeDtypeStruct((s, N_FX), F32),
            jax.ShapeDtypeStruct((s // bs, n_kt, N_FX), F32),
            jax.ShapeDtypeStruct((s // bs, n_kt, N_FX), F32),
        ],
        scratch_shapes=[pltpu.VMEM((1, N_FX), F32), pltpu.VMEM((1, N_FX), F32)],
        compiler_params=pltpu.CompilerParams(
            dimension_semantics=("arbitrary",), vmem_limit_bytes=VMEM_LIMIT),
        name="proj",
    )(x2, wqT, wk, wvT, wf, bf)


def _head_rmsnorm_T(oT, g_col):
    ms = jnp.mean(oT * oT, axis=0, keepdims=True)
    return oT * lax.rsqrt(ms + RMS_EPS) * g_col


def _diag_iotas(sub):
    srow = lax.broadcasted_iota(jnp.int32, (BK, BQ), 0) + sub * BK
    tcol = lax.broadcasted_iota(jnp.int32, (BK, BQ), 1)
    return srow, tcol


def _key_block(k_ref, j):
    return k_ref[pl.ds(pl.multiple_of(j * BK, BK), BK), :]


def _sb_kernel(qT_ref, k_ref, vT_ref, tri_ref, g_ref, o_ref,
               z0_ref, hi0_ref, lo0_ref, w0_ref, r0_ref,
               z1_ref, hi1_ref, lo1_ref, w1_ref, r1_ref,
               acc_ref, run_ref):
    i = pl.program_id(1)
    sets = ((z0_ref, hi0_ref, lo0_ref, w0_ref, r0_ref),
            (z1_ref, hi1_ref, lo1_ref, w1_ref, r1_ref))
    n_tiles = NSUB * (i + 1)

    def key_tile(t):
        return n_tiles - 1 - t

    def step(score_t=None, csum_t=None, pv_t=None, score_diag=None, csum_diag=None):
        if score_t is not None:
            z = jnp.dot(_key_block(k_ref, key_tile(score_t[0])), qT_ref[...],
                        preferred_element_type=F32)
        if csum_t is not None:
            z_ref, hi_ref, lo_ref, w_ref, r_ref = sets[csum_t[1]]
            tri = tri_ref[...]
            csum = (jnp.dot(tri, hi_ref[...], preferred_element_type=F32)
                    + jnp.dot(tri, lo_ref[...], preferred_element_type=F32))
        if pv_t is not None:
            pv_set = sets[pv_t[1]]
            acc_ref[...] += pv_set[4][...] * jnp.dot(
                vT_ref[key_tile(pv_t[0])], pv_set[3][...], preferred_element_type=F32)
        if score_t is not None:
            zs_ref, his_ref, los_ref, _, _ = sets[score_t[1]]
            neg_abs = lax.bitcast_convert_type(
                lax.bitcast_convert_type(z, jnp.uint32) | jnp.uint32(0x80000000), F32)
            sp = jnp.maximum(z, 0.0) + jnp.log(1.0 + jnp.exp2(neg_abs)) * LOG2E
            if score_diag is not None:
                srow, tcol = _diag_iotas(score_diag)
                sp = jnp.where(srow < tcol, sp, 0.0)
            hi = sp.astype(BF16)
            zs_ref[...] = z
            his_ref[...] = hi
            los_ref[...] = (sp - hi.astype(F32)).astype(BF16)
        if csum_t is not None:
            w = jnp.exp2(z_ref[...] - csum)
            if csum_diag is not None:
                srow, tcol = _diag_iotas(csum_diag)
                w = jnp.where(srow < tcol, w, 0.0)
            w_ref[...] = w.astype(BF16)
            run = run_ref[...]
            r_ref[...] = jnp.exp2(-run)
            run_ref[...] = run + csum[0:1, :]

    def alive():
        return (jnp.min(run_ref[...]) < SB_DEAD_LOG2).astype(jnp.int32)

    acc_ref[...] = jnp.zeros_like(acc_ref)
    run_ref[...] = jnp.zeros_like(run_ref)

    step(score_t=(0, 0), score_diag=1)
    step(score_t=(1, 1), score_diag=0, csum_t=(0, 0), csum_diag=1)

    @pl.when(i == 0)
    def _():
        step(csum_t=(1, 1), csum_diag=0, pv_t=(0, 0))
        step(pv_t=(1, 1))

    @pl.when(i > 0)
    def _():
        step(score_t=(2, 0), csum_t=(1, 1), csum_diag=0, pv_t=(0, 0))
        step(score_t=(3, 1), csum_t=(2, 0), pv_t=(1, 1))

        def cond(carry):
            return jnp.logical_and(carry[0] < n_tiles, carry[1] > 0)

        def body(carry):
            t = carry[0]
            still_alive = alive()
            step(score_t=(t, 0), csum_t=(t - 1, 1), pv_t=(t - 2, 0))
            step(score_t=(t + 1, 1), csum_t=(t, 0), pv_t=(t - 1, 1))
            return t + 2, still_alive

        t_end, _ = lax.while_loop(cond, body, (jnp.int32(4), alive()))
        step(csum_t=(t_end - 1, 1), pv_t=(t_end - 2, 0))
        step(pv_t=(t_end - 1, 1))

    o_ref[...] = _head_rmsnorm_T(acc_ref[...], g_ref[...]).astype(o_ref.dtype)


def _sb_scratch():
    tile_f32 = pltpu.VMEM((BK, BQ), F32)
    tile_bf16 = pltpu.VMEM((BK, BQ), BF16)
    row = pltpu.VMEM((1, BQ), F32)
    one_set = [tile_f32, tile_bf16, tile_bf16, tile_bf16, row]
    return one_set + one_set + [pltpu.VMEM((HEAD_DIM, BQ), F32), row]


def _fx_kernel(kpre_ref, cend_ref, qT_ref, k_ref, vT_ref, ct_ref, g_ref, o_ref,
               u0_ref, mx0_ref, p0_ref, al0_ref, u1_ref, mx1_ref, p1_ref, al1_ref,
               acc_ref, m_ref, l_ref, qn_ref):
    h = pl.program_id(0)
    i = pl.program_id(1)
    sets = ((u0_ref, mx0_ref, p0_ref, al0_ref), (u1_ref, mx1_ref, p1_ref, al1_ref))
    n_tiles = NSUB * (i + 1)

    def key_tile(t):
        return n_tiles - 1 - t

    def step(score_t=None, soft_t=None, pv_t=None, score_diag=None):
        if score_t is not None:
            u = jnp.dot(_key_block(k_ref, score_t[0]), qT_ref[...],
                        preferred_element_type=F32)
        if pv_t is not None:
            pv_set = sets[pv_t[1]]
            acc_ref[...] = pv_set[3][...] * acc_ref[...] + jnp.dot(
                vT_ref[pv_t[0]], pv_set[2][...], preferred_element_type=F32)
        if score_t is not None:
            us_ref, mxs_ref, _, _ = sets[score_t[1]]
            if score_diag is not None:
                srow, tcol = _diag_iotas(score_diag)
                u = jnp.where(srow <= tcol, u, -jnp.inf)
            us_ref[...] = u
            mxs_ref[...] = jnp.max(u, axis=0, keepdims=True)
        if soft_t is not None:
            u_ref, mx_ref, p_ref, al_ref = sets[soft_t[1]]
            ct = ct_ref[...]
            m_old = m_ref[...]
            m_new = jnp.maximum(m_old, mx_ref[...] + ct)
            p = jnp.exp(u_ref[...] + (ct - m_new))
            alpha = jnp.exp(m_old - m_new)
            m_ref[...] = m_new
            l_ref[...] = alpha * l_ref[...] + jnp.sum(p, axis=0, keepdims=True)
            p_ref[...] = p.astype(BF16)
            al_ref[...] = alpha

    def alive(t):
        j = jnp.maximum(key_tile(t), 0)
        kmax = kpre_ref[h, j]
        c_end = cend_ref[h, j]
        bound = (qn_ref[...] * (kmax * FX_NORM_SLACK)
                 + (ct_ref[...] - c_end) - m_ref[...])
        return (jnp.max(bound) >= -(FX_DEAD_LN + 1e-5 * jnp.abs(c_end))).astype(jnp.int32)

    acc_ref[...] = jnp.zeros_like(acc_ref)
    l_ref[...] = jnp.zeros_like(l_ref)
    m_ref[...] = jnp.full(m_ref.shape, -jnp.inf, F32)
    qf = qT_ref[0:HEAD_DIM, :].astype(F32)
    qn_ref[...] = jnp.sqrt(jnp.sum(qf * qf, axis=0, keepdims=True))

    d0 = NSUB * i
    step(score_t=(d0, 0), score_diag=0)
    step(score_t=(d0 + 1, 1), score_diag=1, soft_t=(d0, 0))

    @pl.when(i == 0)
    def _():
        step(soft_t=(d0 + 1, 1), pv_t=(d0, 0))
        step(pv_t=(d0 + 1, 1))

    @pl.when(i > 0)
    def _():
        step(score_t=(key_tile(2), 0), soft_t=(d0 + 1, 1), pv_t=(d0, 0))
        step(score_t=(key_tile(3), 1), soft_t=(key_tile(2), 0), pv_t=(d0 + 1, 1))

        def cond(carry):
            return jnp.logical_and(carry[0] < n_tiles, carry[1] > 0)

        def body(carry):
            t = carry[0]
            next_alive = alive(t + 2)
            step(score_t=(key_tile(t), 0), soft_t=(key_tile(t - 1), 1),
                 pv_t=(key_tile(t - 2), 0))
            step(score_t=(key_tile(t + 1), 1), soft_t=(key_tile(t), 0),
                 pv_t=(key_tile(t - 1), 1))
            return t + 2, next_alive

        t_end, _ = lax.while_loop(cond, body, (jnp.int32(4), alive(4)))
        step(soft_t=(key_tile(t_end - 1), 1), pv_t=(key_tile(t_end - 2), 0))
        step(pv_t=(key_tile(t_end - 1), 1))

    o_ref[...] = _head_rmsnorm_T(acc_ref[...] / l_ref[...], g_ref[...]).astype(o_ref.dtype)


def _fx_scratch():
    tile_f32 = pltpu.VMEM((BK, BQ), F32)
    tile_bf16 = pltpu.VMEM((BK, BQ), BF16)
    row = pltpu.VMEM((1, BQ), F32)
    one_set = [tile_f32, row, tile_bf16, row]
    return one_set + one_set + [pltpu.VMEM((HEAD_DIM, BQ), F32), row, row, row]


def _attention(kernel_fn, head0, smem_inputs, qT, k, vT, extra, extra_spec, g_col,
               scratch, name):
    s = k.shape[0]
    nq = s // BQ
    nkb = s // BK
    smem_spec = pl.BlockSpec(memory_space=pltpu.SMEM)
    return pl.pallas_call(
        kernel_fn,
        grid=(N_SB, nq),
        in_specs=[smem_spec] * len(smem_inputs) + [
            pl.BlockSpec((HEAD_PAD, BQ), lambda h, i: (head0 + h, i)),
            pl.BlockSpec((s, HEAD_PAD), lambda h, i: (0, head0 + h)),
            pl.BlockSpec((nkb, HEAD_DIM, BK), lambda h, i: (0, head0 + h, 0)),
            extra_spec,
            pl.BlockSpec((HEAD_DIM, 1), lambda h, i: (h, 0)),
        ],
        out_specs=pl.BlockSpec((HEAD_DIM, BQ), lambda h, i: (h, i)),
        out_shape=jax.ShapeDtypeStruct((N_SB * HEAD_DIM, s), BF16),
        scratch_shapes=scratch,
        compiler_params=pltpu.CompilerParams(
            dimension_semantics=("arbitrary", "arbitrary"),
            vmem_limit_bytes=VMEM_LIMIT),
        name=name,
    )(*smem_inputs, qT, k, vT, extra, g_col)


def _layer_norm(v, g, b):
    mu = jnp.mean(v, axis=-1, keepdims=True)
    vc = v - mu
    var = jnp.mean(vc * vc, axis=-1, keepdims=True)
    return vc * lax.rsqrt(var + LN_EPS) * g + b


def _ffn_kernel(alpha, x_ref, osb_ref, ofx_ref, woa_ref, wob_ref,
                g1_ref, b1_ref, g2_ref, b2_ref, wgu_ref, wd_ref,
                y_ref, act_ref):
    d_ff = wd_ref.shape[0]
    mix = (lax.dot_general(osb_ref[...], woa_ref[...], _TN, preferred_element_type=F32)
           + lax.dot_general(ofx_ref[...], wob_ref[...], _TN, preferred_element_type=F32))
    h1 = _layer_norm(alpha * x_ref[...] + mix, g1_ref[...], b1_ref[...])
    h1b = h1.astype(BF16)
    for c in range(d_ff // FF_CHUNK):
        lo, hi = c * FF_CHUNK, (c + 1) * FF_CHUNK
        gate = jnp.dot(h1b, wgu_ref[:, lo:hi], preferred_element_type=F32)
        up = jnp.dot(h1b, wgu_ref[:, d_ff + lo:d_ff + hi], preferred_element_type=F32)
        act_ref[:, lo:hi] = (gate * jax.nn.sigmoid(gate) * up).astype(BF16)
    ff = jnp.dot(act_ref[...], wd_ref[...], preferred_element_type=F32)
    y_ref[...] = _layer_norm(alpha * h1 + ff, g2_ref[...], b2_ref[...])


def _out_ffn(alpha, x2, oT_sb, oT_fx, wo_a, wo_b, g1, b1, g2, b2, wgu, wd):
    s, dm = x2.shape
    bs = BLOCK_ROWS
    d_ff = wd.shape[0]
    half = oT_sb.shape[0]
    return pl.pallas_call(
        functools.partial(_ffn_kernel, alpha),
        grid=(s // bs,),
        in_specs=[
            pl.BlockSpec((bs, dm), lambda i: (i, 0)),
            pl.BlockSpec((half, bs), lambda i: (0, i)),
            pl.BlockSpec((half, bs), lambda i: (0, i)),
            _const_spec(wo_a.shape),
            _const_spec(wo_b.shape),
            _const_spec(g1.shape),
            _const_spec(b1.shape),
            _const_spec(g2.shape),
            _const_spec(b2.shape),
            _const_spec(wgu.shape),
            _const_spec(wd.shape),
        ],
        out_specs=pl.BlockSpec((bs, dm), lambda i: (i, 0)),
        out_shape=jax.ShapeDtypeStruct((s, dm), F32),
        scratch_shapes=[pltpu.VMEM((bs, d_ff), BF16)],
        compiler_params=pltpu.CompilerParams(
            dimension_semantics=("arbitrary",), vmem_limit_bytes=VMEM_LIMIT),
        name="out_ffn",
    )(x2, oT_sb, oT_fx, wo_a, wo_b, g1, b1, g2, b2, wgu, wd)


def _layer(x2, w_in, b_f, g_sb, g_fox, w_out, ln1_g, ln1_b, ln2_g, ln2_b,
           w_gate_up, w_down, alpha):
    assert NSUB == 2
    s, dm = x2.shape
    sbw = N_SB * HEAD_DIM
    fxw = N_FX * HEAD_DIM
    q_sb, k_sb, v_sb = w_in[:, :sbw], w_in[:, sbw:2 * sbw], w_in[:, 2 * sbw:3 * sbw]
    o = 3 * sbw
    q_fx, k_fx, v_fx = w_in[:, o:o + fxw], w_in[:, o + fxw:o + 2 * fxw], w_in[:, o + 2 * fxw:o + 3 * fxw]
    wf = w_in[:, o + 3 * fxw:]

    wqT = jnp.concatenate([q_sb, q_fx], axis=1).T.astype(BF16)
    wk = jnp.concatenate([k_sb, k_fx], axis=1).reshape(dm, N_HEADS, HEAD_DIM)
    wk = jnp.pad(wk, ((0, 0), (0, 0), (0, HEAD_PAD - HEAD_DIM)))
    wk = wk.reshape(dm, N_HEADS * HEAD_PAD).astype(BF16)
    wvT = jnp.concatenate([v_sb, v_fx], axis=1).T.astype(BF16)

    qT, k, vT, c, kpre, cend = _projection(x2, wqT, wk, wvT, wf, b_f.reshape(1, N_FX))

    srow = lax.broadcasted_iota(jnp.int32, (BK, BK), 0)
    jcol = lax.broadcasted_iota(jnp.int32, (BK, BK), 1)
    tri = (jcol >= srow).astype(BF16)
    ct = c.T.reshape(N_FX, 1, s)
    kpre = kpre.reshape(s // BK, N_FX).T
    cend = cend.reshape(s // BK, N_FX).T

    oT_sb = _attention(
        _sb_kernel, 0, (), qT, k, vT, tri,
        pl.BlockSpec((BK, BK), lambda h, i: (0, 0)),
        g_sb.reshape(sbw, 1), _sb_scratch(), "sb_attn")
    oT_fx = _attention(
        _fx_kernel, N_SB, (kpre, cend), qT, k, vT, ct,
        pl.BlockSpec((None, 1, BQ), lambda h, i: (h, 0, i)),
        g_fox.reshape(fxw, 1), _fx_scratch(), "fx_attn")

    wo = w_out.astype(BF16)
    return _out_ffn(
        alpha, x2, oT_sb, oT_fx, wo[:sbw], wo[sbw:],
        ln1_g.reshape(1, dm), ln1_b.reshape(1, dm),
        ln2_g.reshape(1, dm), ln2_b.reshape(1, dm),
        w_gate_up.astype(BF16), w_down.astype(BF16))


def kernel(x, w_in, b_f, g_sb, g_fox, w_out, ln1_g, ln1_b, ln2_g, ln2_b, w_gate_up, w_down):
    batch, s, dm = x.shape
    depth = w_in.shape[0]
    alpha = (2 * depth) ** 0.25
    outs = []
    for b in range(batch):
        h = x[b]
        for l in range(depth):
            h = _layer(h, w_in[l], b_f[l], g_sb[l], g_fox[l], w_out[l],
                       ln1_g[l], ln1_b[l], ln2_g[l], ln2_b[l],
                       w_gate_up[l], w_down[l], alpha)
        outs.append(h)
    return outs[0][None] if batch == 1 else jnp.stack(outs, axis=0)
```

```python
import functools

import jax
import jax.numpy as jnp
from jax import lax
from jax.experimental import pallas as pl
from jax.experimental.pallas import tpu as pltpu

F32 = jnp.float32
BF16 = jnp.bfloat16

HEAD_DIM = 64
HEAD_PAD = 128
N_SB = 8
N_FX = 8
N_HEADS = N_SB + N_FX
LN_EPS = 1e-5
RMS_EPS = 1e-6
LOG2E = 1.4426950408889634

BLOCK_ROWS = 512
BQ = 512
BK = 256
NSUB = BQ // BK
FF_CHUNK = 256
VMEM_LIMIT = 56 * 1024 * 1024

SB_DEAD_LOG2 = 160.0
FX_DEAD_LN = 110.0
FX_NORM_SLACK = 1.001

_NT = (((1,), (1,)), ((), ()))
_TN = (((0,), (0,)), ((), ()))


def _const_spec(shape):
    nd = len(shape)
    return pl.BlockSpec(shape, lambda *_: (0,) * nd, pipeline_mode=pl.Buffered(1))


def _proj_kernel(x_ref, wqT_ref, wk_ref, wvT_ref, wf_ref, bf_ref,
                 qT_ref, k_ref, vT_ref, c_ref, kpre_ref, cend_ref,
                 carry_ref, kmax_ref):
    i = pl.program_id(0)
    bs = x_ref.shape[0]
    n_kt = bs // BK

    @pl.when(i == 0)
    def _():
        carry_ref[...] = jnp.zeros_like(carry_ref)
        kmax_ref[...] = jnp.zeros_like(kmax_ref)

    xb = x_ref[...]
    xbf = xb.astype(BF16)

    x_lo = (xb - xbf.astype(F32)).astype(BF16)
    wf2 = wf_ref[...]
    f_hi = jnp.dot(xbf, wf2, preferred_element_type=F32)
    f_lo = jnp.dot(x_lo, wf2[:, :N_FX], preferred_element_type=F32)
    f = f_hi[:, :N_FX] + f_hi[:, N_FX:] + f_lo + bf_ref[...]
    logf = jnp.minimum(f, 0.0) - jnp.log1p(jnp.exp(-jnp.abs(f)))
    row = lax.broadcasted_iota(jnp.int32, logf.shape, 0)
    c = logf
    d = 1
    while d < bs:
        c = c + jnp.where(row >= d, pltpu.roll(c, d, axis=0), 0.0)
        d *= 2
    c = c + carry_ref[...]
    carry_ref[...] = c[bs - 1:bs, :]
    c_ref[...] = c
    stat_tile = lax.broadcasted_iota(jnp.int32, (n_kt, N_FX), 0)
    stat_head = lax.broadcasted_iota(jnp.int32, (n_kt, N_FX), 1)
    c_end = jnp.zeros((n_kt, N_FX), F32)
    for t in range(n_kt):
        last = jnp.broadcast_to(c[(t + 1) * BK - 1:(t + 1) * BK, :], (n_kt, N_FX))
        c_end = jnp.where(stat_tile == t, last, c_end)
    cend_ref[0] = c_end

    lane = lax.broadcasted_iota(jnp.int32, (bs, HEAD_PAD), 1)
    knorm2 = jnp.zeros((n_kt, N_FX), F32)
    for hp in range(N_HEADS // 2):
        r = jnp.dot(xbf, wk_ref[:, hp * 256:(hp + 1) * 256],
                    preferred_element_type=F32)
        for sub in range(2):
            h = 2 * hp + sub
            rh = r[:, sub * HEAD_PAD:(sub + 1) * HEAD_PAD]
            if h >= N_SB:
                hh = h - N_SB
                kr = rh.astype(BF16).astype(F32)
                n2 = jnp.sum(kr * kr, axis=1, keepdims=True)
                for t in range(n_kt):
                    tmax = jnp.max(n2[t * BK:(t + 1) * BK, :], axis=0, keepdims=True)
                    knorm2 = jnp.where((stat_tile == t) & (stat_head == hh),
                                       jnp.broadcast_to(tmax, (n_kt, N_FX)), knorm2)
                cb = jnp.broadcast_to(c[:, hh:hh + 1], (bs, HEAD_PAD))
                hi = cb.astype(BF16).astype(F32)
                r1 = cb - hi
                mid = r1.astype(BF16).astype(F32)
                lo = (r1 - mid).astype(BF16).astype(F32)
                aug = jnp.where(lane == HEAD_DIM, hi,
                                jnp.where(lane == HEAD_DIM + 1, mid,
                                          jnp.where(lane == HEAD_DIM + 2, lo, 0.0)))
                rh = rh + aug
            k_ref[:, h * HEAD_PAD:(h + 1) * HEAD_PAD] = rh.astype(BF16)

    knorm = jnp.sqrt(knorm2)
    k_pre = jnp.zeros((n_kt, N_FX), F32)
    run_max = kmax_ref[...]
    for t in range(n_kt):
        run_max = jnp.maximum(run_max, knorm[t:t + 1, :])
        k_pre = jnp.where(stat_tile == t, jnp.broadcast_to(run_max, (n_kt, N_FX)), k_pre)
    kmax_ref[...] = run_max
    kpre_ref[0] = k_pre

    pad_rows = HEAD_PAD - HEAD_DIM
    prow = lax.broadcasted_iota(jnp.int32, (pad_rows, bs), 0)
    neg_rows = jnp.where(prow < 3, -1.0, 0.0).astype(BF16)
    zero_rows = jnp.zeros((pad_rows, bs), BF16)
    for cidx in range(4):
        r = lax.dot_general(wqT_ref[cidx * 256:(cidx + 1) * 256, :], xbf, _NT,
                            preferred_element_type=F32)
        for sub in range(4):
            h = cidx * 4 + sub
            base = h * HEAD_PAD
            scale = HEAD_DIM ** -0.5 * (LOG2E if h < N_SB else 1.0)
            qT_ref[base:base + HEAD_DIM, :] = (
                r[sub * HEAD_DIM:(sub + 1) * HEAD_DIM, :] * scale).astype(BF16)
            qT_ref[base + HEAD_DIM:base + HEAD_PAD, :] = (
                neg_rows if h >= N_SB else zero_rows)

    for cidx in range(4):
        r = lax.dot_general(wvT_ref[cidx * 256:(cidx + 1) * 256, :], xbf, _NT,
                            preferred_element_type=F32)
        for jb in range(n_kt):
            vT_ref[jb, cidx * 256:(cidx + 1) * 256, :] = (
                r[:, jb * BK:(jb + 1) * BK].astype(BF16))


def _projection(x2, wqT, wk, wvT, wf, bf):
    s, dm = x2.shape
    bs = BLOCK_ROWS
    n_kt = bs // BK
    return pl.pallas_call(
        _proj_kernel,
        grid=(s // bs,),
        in_specs=[
            pl.BlockSpec((bs, dm), lambda i: (i, 0)),
            _const_spec(wqT.shape),
            _const_spec(wk.shape),
            _const_spec(wvT.shape),
            _const_spec(wf.shape),
            _const_spec(bf.shape),
        ],
        out_specs=[
            pl.BlockSpec((N_HEADS * HEAD_PAD, bs), lambda i: (0, i)),
            pl.BlockSpec((bs, N_HEADS * HEAD_PAD), lambda i: (i, 0)),
            pl.BlockSpec((n_kt, N_HEADS * HEAD_DIM, BK), lambda i: (i, 0, 0)),
            pl.BlockSpec((bs, N_FX), lambda i: (i, 0)),
            pl.BlockSpec((1, n_kt, N_FX), lambda i: (i, 0, 0)),
            pl.BlockSpec((1, n_kt, N_FX), lambda i: (i, 0, 0)),
        ],
        out_shape=[
            jax.ShapeDtypeStruct((N_HEADS * HEAD_PAD, s), BF16),
            jax.ShapeDtypeStruct((s, N_HEADS * HEAD_PAD), BF16),
            jax.ShapeDtypeStruct((s // BK, N_HEADS * HEAD_DIM, BK), BF16),
            jax.ShapeDtypeStruct((s, N_FX), F32),
            jax.ShapeDtypeStruct((s // bs, n_kt, N_FX), F32),
            jax.ShapeDtypeStruct((s // bs, n_kt, N_FX), F32),
        ],
        scratch_shapes=[pltpu.VMEM((1, N_FX), F32), pltpu.VMEM((1, N_FX), F32)],
        compiler_params=pltpu.CompilerParams(
            dimension_semantics=("arbitrary",), vmem_limit_bytes=VMEM_LIMIT),
        name="proj",
    )(x2, wqT, wk, wvT, wf, bf)


def _head_rmsnorm_T(oT, g_col):
    ms = jnp.mean(oT * oT, axis=0, keepdims=True)
    return oT * lax.rsqrt(ms + RMS_EPS) * g_col


def _key_before_query(n_queries, strict):
    srow = lax.broadcasted_iota(jnp.int32, (BK, n_queries), 0)
    tcol = lax.broadcasted_iota(jnp.int32, (BK, n_queries), 1)
    return srow < tcol if strict else srow <= tcol


def _key_block(k_ref, j):
    return k_ref[pl.ds(pl.multiple_of(j * BK, BK), BK), :]


def _sb_kernel(qT_ref, k_ref, vT_ref, tri_ref, g_ref, o_ref,
               z0_ref, hi0_ref, lo0_ref, w0_ref, r0_ref,
               z1_ref, hi1_ref, lo1_ref, w1_ref, r1_ref,
               acc_ref, run_ref):
    i = pl.program_id(1)
    sets = ((z0_ref, hi0_ref, lo0_ref, w0_ref, r0_ref),
            (z1_ref, hi1_ref, lo1_ref, w1_ref, r1_ref))
    n_tiles = NSUB * (i + 1)

    def key_tile(t):
        return n_tiles - 1 - t

    def softplus2(z):
        return jnp.maximum(z, 0.0) + jnp.log(1.0 + jnp.exp2(-jnp.abs(z))) * LOG2E

    def split(sp):
        hi = sp.astype(BF16)
        return hi, (sp - hi.astype(F32)).astype(BF16)

    def suffix_sums(hi, lo):
        tri = tri_ref[...]
        return (jnp.dot(tri, hi, preferred_element_type=F32)
                + jnp.dot(tri, lo, preferred_element_type=F32))

    def batch(descs):
        zs = [jnp.dot(_key_block(k_ref, key_tile(t)), qT_ref[:, q0:],
                      preferred_element_type=F32) for t, q0, _ in descs]
        parts = []
        for (_, _, mask), z in zip(descs, zs):
            sp = softplus2(z)
            if mask is not None:
                sp = jnp.where(mask, sp, 0.0)
            parts.append(split(sp))
        csums = [suffix_sums(hi, lo) for hi, lo in parts]
        ws = []
        for (_, _, mask), z, csum in zip(descs, zs, csums):
            w = jnp.exp2(z - csum)
            if mask is not None:
                w = jnp.where(mask, w, 0.0)
            ws.append(w.astype(BF16))
        pvs = [jnp.dot(vT_ref[key_tile(t)], w, preferred_element_type=F32)
               for (t, _, _), w in zip(descs, ws)]
        run = run_ref[...]
        acc = acc_ref[...]
        for (_, q0, _), csum, pv in zip(descs, csums, pvs):
            total = csum[0:1, :]
            if q0:
                total = jnp.concatenate([jnp.zeros((1, q0), F32), total], axis=1)
                pv = jnp.concatenate([jnp.zeros((HEAD_DIM, q0), F32), pv], axis=1)
            acc = acc + pv * jnp.exp2(-run)
            run = run + total
        acc_ref[...] = acc
        run_ref[...] = run

    def step(score_t=None, csum_t=None, pv_t=None):
        if score_t is not None:
            z = jnp.dot(_key_block(k_ref, key_tile(score_t[0])), qT_ref[...],
                        preferred_element_type=F32)
        if csum_t is not None:
            z_ref, hi_ref, lo_ref, w_ref, r_ref = sets[csum_t[1]]
            csum = suffix_sums(hi_ref[...], lo_ref[...])
        if pv_t is not None:
            pv_set = sets[pv_t[1]]
            acc_ref[...] += pv_set[4][...] * jnp.dot(
                vT_ref[key_tile(pv_t[0])], pv_set[3][...], preferred_element_type=F32)
        if score_t is not None:
            zs_ref, his_ref, los_ref, _, _ = sets[score_t[1]]
            hi, lo = split(softplus2(z))
            zs_ref[...] = z
            his_ref[...] = hi
            los_ref[...] = lo
        if csum_t is not None:
            w_ref[...] = jnp.exp2(z_ref[...] - csum).astype(BF16)
            run = run_ref[...]
            r_ref[...] = jnp.exp2(-run)
            run_ref[...] = run + csum[0:1, :]

    def alive():
        return jnp.min(run_ref[...]) < SB_DEAD_LOG2

    acc_ref[...] = jnp.zeros_like(acc_ref)
    run_ref[...] = jnp.zeros_like(run_ref)

    mask_lo = _key_before_query(BQ, strict=True)
    mask_hi = _key_before_query(BQ - BK, strict=True)
    diag = [(0, BK, mask_hi), (1, 0, mask_lo)]

    @pl.when(i == 0)
    def _():
        batch(diag)

    @pl.when(i > 0)
    def _():
        batch(diag + [(2, 0, None)])

        @pl.when(alive())
        def _():
            batch([(3, 0, None)])

            @pl.when(jnp.logical_and(alive(), n_tiles > 4))
            def _():
                step(score_t=(4, 0))
                step(score_t=(5, 1), csum_t=(4, 0))

                def cond(carry):
                    return jnp.logical_and(carry[0] < n_tiles, carry[1] > 0)

                def body(carry):
                    t = carry[0]
                    still_alive = alive().astype(jnp.int32)
                    step(score_t=(t, 0), csum_t=(t - 1, 1), pv_t=(t - 2, 0))
                    step(score_t=(t + 1, 1), csum_t=(t, 0), pv_t=(t - 1, 1))
                    return t + 2, still_alive

                t_end, _ = lax.while_loop(cond, body, (jnp.int32(6), jnp.int32(1)))
                step(csum_t=(t_end - 1, 1), pv_t=(t_end - 2, 0))
                step(pv_t=(t_end - 1, 1))

    o_ref[...] = _head_rmsnorm_T(acc_ref[...], g_ref[...]).astype(o_ref.dtype)


def _sb_scratch():
    tile_f32 = pltpu.VMEM((BK, BQ), F32)
    tile_bf16 = pltpu.VMEM((BK, BQ), BF16)
    row = pltpu.VMEM((1, BQ), F32)
    one_set = [tile_f32, tile_bf16, tile_bf16, tile_bf16, row]
    return one_set + one_set + [pltpu.VMEM((HEAD_DIM, BQ), F32), row]


def _fx_kernel(kpre_ref, cend_ref, qT_ref, k_ref, vT_ref, ct_ref, g_ref, o_ref,
               u0_ref, mx0_ref, p0_ref, al0_ref, u1_ref, mx1_ref, p1_ref, al1_ref,
               acc_ref, m_ref, l_ref, qn_ref):
    h = pl.program_id(0)
    i = pl.program_id(1)
    sets = ((u0_ref, mx0_ref, p0_ref, al0_ref), (u1_ref, mx1_ref, p1_ref, al1_ref))
    n_tiles = NSUB * (i + 1)

    def key_tile(t):
        return n_tiles - 1 - t

    def batch(groups):
        scores = [[jnp.dot(_key_block(k_ref, j), qT_ref[:, q0:],
                           preferred_element_type=F32) for j, q0, _ in descs]
                  for descs in groups]
        for descs, us in zip(groups, scores):
            soft_group(descs, us)

    def soft_group(descs, us):
        us = [u if mask is None else jnp.where(mask, u, -jnp.inf)
              for (_, _, mask), u in zip(descs, us)]
        mx = None
        for (_, q0, _), u in zip(descs, us):
            cm = jnp.max(u, axis=0, keepdims=True)
            if q0:
                cm = jnp.concatenate([jnp.full((1, q0), -jnp.inf, F32), cm], axis=1)
            mx = cm if mx is None else jnp.maximum(mx, cm)
        ct = ct_ref[...]
        m_old = m_ref[...]
        m_new = jnp.maximum(m_old, mx + ct)
        off = ct - m_new
        alpha = jnp.exp(m_old - m_new)
        l = alpha * l_ref[...]
        acc = alpha * acc_ref[...]
        for (j, q0, _), u in zip(descs, us):
            p = jnp.exp(u + off[:, q0:])
            psum = jnp.sum(p, axis=0, keepdims=True)
            pv = jnp.dot(vT_ref[j], p.astype(BF16), preferred_element_type=F32)
            if q0:
                psum = jnp.concatenate([jnp.zeros((1, q0), F32), psum], axis=1)
                pv = jnp.concatenate([jnp.zeros((HEAD_DIM, q0), F32), pv], axis=1)
            l = l + psum
            acc = acc + pv
        m_ref[...] = m_new
        l_ref[...] = l
        acc_ref[...] = acc

    def step(score_t=None, soft_t=None, pv_t=None):
        if score_t is not None:
            u = jnp.dot(_key_block(k_ref, score_t[0]), qT_ref[...],
                        preferred_element_type=F32)
        if pv_t is not None:
            pv_set = sets[pv_t[1]]
            acc_ref[...] = pv_set[3][...] * acc_ref[...] + jnp.dot(
                vT_ref[pv_t[0]], pv_set[2][...], preferred_element_type=F32)
        if score_t is not None:
            us_ref, mxs_ref, _, _ = sets[score_t[1]]
            us_ref[...] = u
            mxs_ref[...] = jnp.max(u, axis=0, keepdims=True)
        if soft_t is not None:
            u_ref, mx_ref, p_ref, al_ref = sets[soft_t[1]]
            ct = ct_ref[...]
            m_old = m_ref[...]
            m_new = jnp.maximum(m_old, mx_ref[...] + ct)
            p = jnp.exp(u_ref[...] + (ct - m_new))
            alpha = jnp.exp(m_old - m_new)
            m_ref[...] = m_new
            l_ref[...] = alpha * l_ref[...] + jnp.sum(p, axis=0, keepdims=True)
            p_ref[...] = p.astype(BF16)
            al_ref[...] = alpha

    def alive(t):
        j = jnp.maximum(key_tile(t), 0)
        kmax = kpre_ref[h, j]
        c_end = cend_ref[h, j]
        bound = (qn_ref[...] * (kmax * FX_NORM_SLACK)
                 + (ct_ref[...] - c_end) - m_ref[...])
        return jnp.max(bound) >= -(FX_DEAD_LN + 1e-5 * jnp.abs(c_end))

    acc_ref[...] = jnp.zeros_like(acc_ref)
    l_ref[...] = jnp.zeros_like(l_ref)
    m_ref[...] = jnp.full(m_ref.shape, -jnp.inf, F32)
    qf = qT_ref[0:HEAD_DIM, :].astype(F32)
    qn_ref[...] = jnp.sqrt(jnp.sum(qf * qf, axis=0, keepdims=True))

    mask_lo = _key_before_query(BQ, strict=False)
    mask_hi = _key_before_query(BQ - BK, strict=False)
    d0 = NSUB * i
    diag = [(d0, 0, mask_lo), (d0 + 1, BK, mask_hi)]

    @pl.when(i == 0)
    def _():
        batch([diag])

    @pl.when(i > 0)
    def _():
        batch([diag, [(key_tile(2), 0, None), (key_tile(3), 0, None)]])

        @pl.when(jnp.logical_and(alive(4), n_tiles > 4))
        def _():
            step(score_t=(key_tile(4), 0))
            step(score_t=(key_tile(5), 1), soft_t=(key_tile(4), 0))

            def cond(carry):
                return jnp.logical_and(carry[0] < n_tiles, carry[1] > 0)

            def body(carry):
                t = carry[0]
                next_alive = alive(t + 2).astype(jnp.int32)
                step(score_t=(key_tile(t), 0), soft_t=(key_tile(t - 1), 1),
                     pv_t=(key_tile(t - 2), 0))
                step(score_t=(key_tile(t + 1), 1), soft_t=(key_tile(t), 0),
                     pv_t=(key_tile(t - 1), 1))
                return t + 2, next_alive

            t_end, _ = lax.while_loop(cond, body, (jnp.int32(6), alive(6).astype(jnp.int32)))
            step(soft_t=(key_tile(t_end - 1), 1), pv_t=(key_tile(t_end - 2), 0))
            step(pv_t=(key_tile(t_end - 1), 1))

    o_ref[...] = _head_rmsnorm_T(acc_ref[...] / l_ref[...], g_ref[...]).astype(o_ref.dtype)


def _fx_scratch():
    tile_f32 = pltpu.VMEM((BK, BQ), F32)
    tile_bf16 = pltpu.VMEM((BK, BQ), BF16)
    row = pltpu.VMEM((1, BQ), F32)
    one_set = [tile_f32, row, tile_bf16, row]
    return one_set + one_set + [pltpu.VMEM((HEAD_DIM, BQ), F32), row, row, row]


def _attention(kernel_fn, head0, smem_inputs, qT, k, vT, extra, extra_spec, g_col,
               scratch, name):
    s = k.shape[0]
    nq = s // BQ
    nkb = s // BK
    smem_spec = pl.BlockSpec(memory_space=pltpu.SMEM)
    return pl.pallas_call(
        kernel_fn,
        grid=(N_SB, nq),
        in_specs=[smem_spec] * len(smem_inputs) + [
            pl.BlockSpec((HEAD_PAD, BQ), lambda h, i: (head0 + h, i)),
            pl.BlockSpec((s, HEAD_PAD), lambda h, i: (0, head0 + h)),
            pl.BlockSpec((nkb, HEAD_DIM, BK), lambda h, i: (0, head0 + h, 0)),
            extra_spec,
            pl.BlockSpec((HEAD_DIM, 1), lambda h, i: (h, 0)),
        ],
        out_specs=pl.BlockSpec((HEAD_DIM, BQ), lambda h, i: (h, i)),
        out_shape=jax.ShapeDtypeStruct((N_SB * HEAD_DIM, s), BF16),
        scratch_shapes=scratch,
        compiler_params=pltpu.CompilerParams(
            dimension_semantics=("arbitrary", "arbitrary"),
            vmem_limit_bytes=VMEM_LIMIT),
        name=name,
    )(*smem_inputs, qT, k, vT, extra, g_col)


def _layer_norm(v, g, b):
    mu = jnp.mean(v, axis=-1, keepdims=True)
    vc = v - mu
    var = jnp.mean(vc * vc, axis=-1, keepdims=True)
    return vc * lax.rsqrt(var + LN_EPS) * g + b


def _ffn_kernel(alpha, x_ref, osb_ref, ofx_ref, woa_ref, wob_ref,
                g1_ref, b1_ref, g2_ref, b2_ref, wgu_ref, wd_ref,
                y_ref, act_ref):
    d_ff = wd_ref.shape[0]
    mix = (lax.dot_general(osb_ref[...], woa_ref[...], _TN, preferred_element_type=F32)
           + lax.dot_general(ofx_ref[...], wob_ref[...], _TN, preferred_element_type=F32))
    h1 = _layer_norm(alpha * x_ref[...] + mix, g1_ref[...], b1_ref[...])
    h1b = h1.astype(BF16)
    for c in range(d_ff // FF_CHUNK):
        lo, hi = c * FF_CHUNK, (c + 1) * FF_CHUNK
        gate = jnp.dot(h1b, wgu_ref[:, lo:hi], preferred_element_type=F32)
        up = jnp.dot(h1b, wgu_ref[:, d_ff + lo:d_ff + hi], preferred_element_type=F32)
        act_ref[:, lo:hi] = (gate * jax.nn.sigmoid(gate) * up).astype(BF16)
    ff = jnp.dot(act_ref[...], wd_ref[...], preferred_element_type=F32)
    y_ref[...] = _layer_norm(alpha * h1 + ff, g2_ref[...], b2_ref[...])


def _out_ffn(alpha, x2, oT_sb, oT_fx, wo_a, wo_b, g1, b1, g2, b2, wgu, wd):
    s, dm = x2.shape
    bs = BLOCK_ROWS
    d_ff = wd.shape[0]
    half = oT_sb.shape[0]
    return pl.pallas_call(
        functools.partial(_ffn_kernel, alpha),
        grid=(s // bs,),
        in_specs=[
            pl.BlockSpec((bs, dm), lambda i: (i, 0)),
            pl.BlockSpec((half, bs), lambda i: (0, i)),
            pl.BlockSpec((half, bs), lambda i: (0, i)),
            _const_spec(wo_a.shape),
            _const_spec(wo_b.shape),
            _const_spec(g1.shape),
            _const_spec(b1.shape),
            _const_spec(g2.shape),
            _const_spec(b2.shape),
            _const_spec(wgu.shape),
            _const_spec(wd.shape),
        ],
        out_specs=pl.BlockSpec((bs, dm), lambda i: (i, 0)),
        out_shape=jax.ShapeDtypeStruct((s, dm), F32),
        scratch_shapes=[pltpu.VMEM((bs, d_ff), BF16)],
        compiler_params=pltpu.CompilerParams(
            dimension_semantics=("arbitrary",), vmem_limit_bytes=VMEM_LIMIT),
        name="out_ffn",
    )(x2, oT_sb, oT_fx, wo_a, wo_b, g1, b1, g2, b2, wgu, wd)


def _layer(x2, w_in, b_f, g_sb, g_fox, w_out, ln1_g, ln1_b, ln2_g, ln2_b,
           w_gate_up, w_down, alpha):
    assert NSUB == 2
    s, dm = x2.shape
    sbw = N_SB * HEAD_DIM
    fxw = N_FX * HEAD_DIM
    q_sb, k_sb, v_sb = w_in[:, :sbw], w_in[:, sbw:2 * sbw], w_in[:, 2 * sbw:3 * sbw]
    o = 3 * sbw
    q_fx, k_fx, v_fx = w_in[:, o:o + fxw], w_in[:, o + fxw:o + 2 * fxw], w_in[:, o + 2 * fxw:o + 3 * fxw]
    wf = w_in[:, o + 3 * fxw:]

    wqT = jnp.concatenate([q_sb, q_fx], axis=1).T.astype(BF16)
    wk = jnp.concatenate([k_sb, k_fx], axis=1).reshape(dm, N_HEADS, HEAD_DIM)
    wk = jnp.pad(wk, ((0, 0), (0, 0), (0, HEAD_PAD - HEAD_DIM)))
    wk = wk.reshape(dm, N_HEADS * HEAD_PAD).astype(BF16)
    wvT = jnp.concatenate([v_sb, v_fx], axis=1).T.astype(BF16)

    wf_hi = wf.astype(BF16)
    wf_lo = (wf - wf_hi.astype(F32)).astype(BF16)
    wf2 = jnp.concatenate([wf_hi, wf_lo], axis=1)

    qT, k, vT, c, kpre, cend = _projection(x2, wqT, wk, wvT, wf2, b_f.reshape(1, N_FX))

    srow = lax.broadcasted_iota(jnp.int32, (BK, BK), 0)
    jcol = lax.broadcasted_iota(jnp.int32, (BK, BK), 1)
    tri = (jcol >= srow).astype(BF16)
    ct = c.T.reshape(N_FX, 1, s)
    kpre = kpre.reshape(s // BK, N_FX).T
    cend = cend.reshape(s // BK, N_FX).T

    oT_sb = _attention(
        _sb_kernel, 0, (), qT, k, vT, tri,
        pl.BlockSpec((BK, BK), lambda h, i: (0, 0)),
        g_sb.reshape(sbw, 1), _sb_scratch(), "sb_attn")
    oT_fx = _attention(
        _fx_kernel, N_SB, (kpre, cend), qT, k, vT, ct,
        pl.BlockSpec((None, 1, BQ), lambda h, i: (h, 0, i)),
        g_fox.reshape(fxw, 1), _fx_scratch(), "fx_attn")

    wo = w_out.astype(BF16)
    return _out_ffn(
        alpha, x2, oT_sb, oT_fx, wo[:sbw], wo[sbw:],
        ln1_g.reshape(1, dm), ln1_b.reshape(1, dm),
        ln2_g.reshape(1, dm), ln2_b.reshape(1, dm),
        w_gate_up.astype(BF16), w_down.astype(BF16))


def kernel(x, w_in, b_f, g_sb, g_fox, w_out, ln1_g, ln1_b, ln2_g, ln2_b, w_gate_up, w_down):
    batch, s, dm = x.shape
    depth = w_in.shape[0]
    alpha = (2 * depth) ** 0.25
    outs = []
    for b in range(batch):
        h = x[b]
        for l in range(depth):
            h = _layer(h, w_in[l], b_f[l], g_sb[l], g_fox[l], w_out[l],
                       ln1_g[l], ln1_b[l], ln2_g[l], ln2_b[l],
                       w_gate_up[l], w_down[l], alpha)
        outs.append(h)
    return outs[0][None] if batch == 1 else jnp.stack(outs, axis=0)
```

```python
import functools

import jax
import jax.numpy as jnp
from jax import lax
from jax.experimental import pallas as pl
from jax.experimental.pallas import tpu as pltpu

F32 = jnp.float32
BF16 = jnp.bfloat16

HEAD_DIM = 64
HEAD_PAD = 128
N_SB = 8
N_FX = 8
N_HEADS = N_SB + N_FX
LN_EPS = 1e-5
RMS_EPS = 1e-6
LOG2E = 1.4426950408889634

BLOCK_ROWS = 512
BQ = 512
BK = 256
NSUB = BQ // BK
FF_CHUNK = 256
VMEM_LIMIT = 56 * 1024 * 1024

SB_DEAD_LOG2 = 160.0
FX_DEAD_LN = 110.0
FX_NORM_SLACK = 1.001

_NT = (((1,), (1,)), ((), ()))
_TN = (((0,), (0,)), ((), ()))


def _const_spec(shape):
    nd = len(shape)
    return pl.BlockSpec(shape, lambda *_: (0,) * nd, pipeline_mode=pl.Buffered(1))


def _proj_kernel(x_ref, wqT_ref, wk_ref, wvT_ref, wf_ref, bf_ref,
                 qT_ref, k_ref, vT_ref, c_ref, kpre_ref, cend_ref,
                 carry_ref, kmax_ref):
    i = pl.program_id(0)
    bs = x_ref.shape[0]
    n_kt = bs // BK

    @pl.when(i == 0)
    def _():
        carry_ref[...] = jnp.zeros_like(carry_ref)
        kmax_ref[...] = jnp.zeros_like(kmax_ref)

    xb = x_ref[...]
    xbf = xb.astype(BF16)

    x_lo = (xb - xbf.astype(F32)).astype(BF16)
    wf2 = wf_ref[...]
    f_hi = jnp.dot(xbf, wf2, preferred_element_type=F32)
    f_lo = jnp.dot(x_lo, wf2[:, :N_FX], preferred_element_type=F32)
    f = f_hi[:, :N_FX] + f_hi[:, N_FX:] + f_lo + bf_ref[...]
    logf = jnp.minimum(f, 0.0) - jnp.log1p(jnp.exp(-jnp.abs(f)))
    row = lax.broadcasted_iota(jnp.int32, logf.shape, 0)
    c = logf
    d = 1
    while d < bs:
        c = c + jnp.where(row >= d, pltpu.roll(c, d, axis=0), 0.0)
        d *= 2
    c = c + carry_ref[...]
    carry_ref[...] = c[bs - 1:bs, :]
    c_ref[...] = c
    stat_tile = lax.broadcasted_iota(jnp.int32, (n_kt, N_FX), 0)
    stat_head = lax.broadcasted_iota(jnp.int32, (n_kt, N_FX), 1)
    c_end = jnp.zeros((n_kt, N_FX), F32)
    for t in range(n_kt):
        last = jnp.broadcast_to(c[(t + 1) * BK - 1:(t + 1) * BK, :], (n_kt, N_FX))
        c_end = jnp.where(stat_tile == t, last, c_end)
    cend_ref[0] = c_end

    lane = lax.broadcasted_iota(jnp.int32, (bs, HEAD_PAD), 1)
    knorm2 = jnp.zeros((n_kt, N_FX), F32)
    for hp in range(N_HEADS // 2):
        r = jnp.dot(xbf, wk_ref[:, hp * 256:(hp + 1) * 256],
                    preferred_element_type=F32)
        for sub in range(2):
            h = 2 * hp + sub
            rh = r[:, sub * HEAD_PAD:(sub + 1) * HEAD_PAD]
            if h >= N_SB:
                hh = h - N_SB
                kr = rh.astype(BF16).astype(F32)
                n2 = jnp.sum(kr * kr, axis=1, keepdims=True)
                for t in range(n_kt):
                    tmax = jnp.max(n2[t * BK:(t + 1) * BK, :], axis=0, keepdims=True)
                    knorm2 = jnp.where((stat_tile == t) & (stat_head == hh),
                                       jnp.broadcast_to(tmax, (n_kt, N_FX)), knorm2)
                cb = jnp.broadcast_to(c[:, hh:hh + 1], (bs, HEAD_PAD))
                hi = cb.astype(BF16).astype(F32)
                r1 = cb - hi
                mid = r1.astype(BF16).astype(F32)
                lo = (r1 - mid).astype(BF16).astype(F32)
                aug = jnp.where(lane == HEAD_DIM, hi,
                                jnp.where(lane == HEAD_DIM + 1, mid,
                                          jnp.where(lane == HEAD_DIM + 2, lo, 0.0)))
                rh = rh + aug
            k_ref[:, h * HEAD_PAD:(h + 1) * HEAD_PAD] = rh.astype(BF16)

    knorm = jnp.sqrt(knorm2)
    k_pre = jnp.zeros((n_kt, N_FX), F32)
    run_max = kmax_ref[...]
    for t in range(n_kt):
        run_max = jnp.maximum(run_max, knorm[t:t + 1, :])
        k_pre = jnp.where(stat_tile == t, jnp.broadcast_to(run_max, (n_kt, N_FX)), k_pre)
    kmax_ref[...] = run_max
    kpre_ref[0] = k_pre

    pad_rows = HEAD_PAD - HEAD_DIM
    prow = lax.broadcasted_iota(jnp.int32, (pad_rows, bs), 0)
    neg_rows = jnp.where(prow < 3, -1.0, 0.0).astype(BF16)
    zero_rows = jnp.zeros((pad_rows, bs), BF16)
    for cidx in range(4):
        r = lax.dot_general(wqT_ref[cidx * 256:(cidx + 1) * 256, :], xbf, _NT,
                            preferred_element_type=F32)
        for sub in range(4):
            h = cidx * 4 + sub
            base = h * HEAD_PAD
            scale = HEAD_DIM ** -0.5 * (LOG2E if h < N_SB else 1.0)
            qT_ref[base:base + HEAD_DIM, :] = (
                r[sub * HEAD_DIM:(sub + 1) * HEAD_DIM, :] * scale).astype(BF16)
            qT_ref[base + HEAD_DIM:base + HEAD_PAD, :] = (
                neg_rows if h >= N_SB else zero_rows)

    for cidx in range(4):
        r = lax.dot_general(wvT_ref[cidx * 256:(cidx + 1) * 256, :], xbf, _NT,
                            preferred_element_type=F32)
        for jb in range(n_kt):
            vT_ref[jb, cidx * 256:(cidx + 1) * 256, :] = (
                r[:, jb * BK:(jb + 1) * BK].astype(BF16))


def _projection(x2, wqT, wk, wvT, wf, bf):
    s, dm = x2.shape
    bs = BLOCK_ROWS
    n_kt = bs // BK
    return pl.pallas_call(
        _proj_kernel,
        grid=(s // bs,),
        in_specs=[
            pl.BlockSpec((bs, dm), lambda i: (i, 0)),
            _const_spec(wqT.shape),
            _const_spec(wk.shape),
            _const_spec(wvT.shape),
            _const_spec(wf.shape),
            _const_spec(bf.shape),
        ],
        out_specs=[
            pl.BlockSpec((N_HEADS * HEAD_PAD, bs), lambda i: (0, i)),
            pl.BlockSpec((bs, N_HEADS * HEAD_PAD), lambda i: (i, 0)),
            pl.BlockSpec((n_kt, N_HEADS * HEAD_DIM, BK), lambda i: (i, 0, 0)),
            pl.BlockSpec((bs, N_FX), lambda i: (i, 0)),
            pl.BlockSpec((1, n_kt, N_FX), lambda i: (i, 0, 0)),
            pl.BlockSpec((1, n_kt, N_FX), lambda i: (i, 0, 0)),
        ],
        out_shape=[
            jax.ShapeDtypeStruct((N_HEADS * HEAD_PAD, s), BF16),
            jax.ShapeDtypeStruct((s, N_HEADS * HEAD_PAD), BF16),
            jax.ShapeDtypeStruct((s // BK, N_HEADS * HEAD_DIM, BK), BF16),
            jax.ShapeDtypeStruct((s, N_FX), F32),
            jax.ShapeDtypeStruct((s // bs, n_kt, N_FX), F32),
            jax.ShapeDtypeStruct((s // bs, n_kt, N_FX), F32),
        ],
        scratch_shapes=[pltpu.VMEM((1, N_FX), F32), pltpu.VMEM((1, N_FX), F32)],
        compiler_params=pltpu.CompilerParams(
            dimension_semantics=("arbitrary",), vmem_limit_bytes=VMEM_LIMIT),
        name="proj",
    )(x2, wqT, wk, wvT, wf, bf)


def _head_rmsnorm_T(oT, g_col):
    ms = jnp.mean(oT * oT, axis=0, keepdims=True)
    return oT * lax.rsqrt(ms + RMS_EPS) * g_col


def _key_before_query(n_queries, strict):
    srow = lax.broadcasted_iota(jnp.int32, (BK, n_queries), 0)
    tcol = lax.broadcasted_iota(jnp.int32, (BK, n_queries), 1)
    return srow < tcol if strict else srow <= tcol


def _key_block(k_ref, j):
    return k_ref[pl.ds(pl.multiple_of(j * BK, BK), BK), :]


def _dot(a, b):
    return jnp.dot(a, b, preferred_element_type=F32)


def _widen(x, q0, fill):
    if not q0:
        return x
    return jnp.concatenate([jnp.full((x.shape[0], q0), fill, F32), x], axis=1)


class _StickBreaking:
    def __init__(self, i, qT_ref, k_ref, vT_ref, tri_ref, sets, acc_ref, run_ref):
        self.qT_ref, self.k_ref, self.vT_ref, self.tri_ref = qT_ref, k_ref, vT_ref, tri_ref
        self.sets, self.acc_ref, self.run_ref = sets, acc_ref, run_ref
        self.n_tiles = NSUB * (i + 1)

    def key_tile(self, t):
        return self.n_tiles - 1 - t

    def init(self):
        self.acc_ref[...] = jnp.zeros_like(self.acc_ref)
        self.run_ref[...] = jnp.zeros_like(self.run_ref)

    def diag(self):
        return [(0, BK, _key_before_query(BQ - BK, strict=True)),
                (1, 0, _key_before_query(BQ, strict=True))]

    @staticmethod
    def softplus2(z):
        return jnp.maximum(z, 0.0) + jnp.log(1.0 + jnp.exp2(-jnp.abs(z))) * LOG2E

    @staticmethod
    def split(sp):
        hi = sp.astype(BF16)
        return hi, (sp - hi.astype(F32)).astype(BF16)

    def suffix_sums(self, hi, lo):
        tri = self.tri_ref[...]
        return _dot(tri, hi) + _dot(tri, lo)

    def scores(self, descs):
        return [_dot(_key_block(self.k_ref, self.key_tile(t)), self.qT_ref[:, q0:])
                for t, q0, _ in descs]

    def splits(self, descs, zs):
        parts = []
        for (_, _, mask), z in zip(descs, zs):
            sp = self.softplus2(z)
            if mask is not None:
                sp = jnp.where(mask, sp, 0.0)
            parts.append(self.split(sp))
        return parts

    def finish(self, descs, zs, csums):
        ws = []
        for (_, _, mask), z, csum in zip(descs, zs, csums):
            w = jnp.exp2(z - csum)
            if mask is not None:
                w = jnp.where(mask, w, 0.0)
            ws.append(w.astype(BF16))
        pvs = [_dot(self.vT_ref[self.key_tile(t)], w) for (t, _, _), w in zip(descs, ws)]
        run = self.run_ref[...]
        acc = self.acc_ref[...]
        for (_, q0, _), csum, pv in zip(descs, csums, pvs):
            acc = acc + _widen(pv, q0, 0.0) * jnp.exp2(-run)
            run = run + _widen(csum[0:1, :], q0, 0.0)
        self.acc_ref[...] = acc
        self.run_ref[...] = run

    def batch(self, descs):
        zs = self.scores(descs)
        parts = self.splits(descs, zs)
        self.finish(descs, zs, [self.suffix_sums(hi, lo) for hi, lo in parts])

    def step(self, score_t=None, csum_t=None, pv_t=None):
        if score_t is not None:
            z = _dot(_key_block(self.k_ref, self.key_tile(score_t[0])), self.qT_ref[...])
        if csum_t is not None:
            z_ref, hi_ref, lo_ref, w_ref, r_ref = self.sets[csum_t[1]]
            csum = self.suffix_sums(hi_ref[...], lo_ref[...])
        if pv_t is not None:
            pv_set = self.sets[pv_t[1]]
            self.acc_ref[...] += pv_set[4][...] * _dot(
                self.vT_ref[self.key_tile(pv_t[0])], pv_set[3][...])
        if score_t is not None:
            zs_ref, his_ref, los_ref, _, _ = self.sets[score_t[1]]
            hi, lo = self.split(self.softplus2(z))
            zs_ref[...] = z
            his_ref[...] = hi
            los_ref[...] = lo
        if csum_t is not None:
            w_ref[...] = jnp.exp2(z_ref[...] - csum).astype(BF16)
            run = self.run_ref[...]
            r_ref[...] = jnp.exp2(-run)
            self.run_ref[...] = run + csum[0:1, :]

    def alive(self):
        return jnp.min(self.run_ref[...]) < SB_DEAD_LOG2

    def rest(self):
        @pl.when(self.alive())
        def _():
            self.batch([(3, 0, None)])

            @pl.when(jnp.logical_and(self.alive(), self.n_tiles > 4))
            def _():
                self.step(score_t=(4, 0))
                self.step(score_t=(5, 1), csum_t=(4, 0))

                def cond(carry):
                    return jnp.logical_and(carry[0] < self.n_tiles, carry[1] > 0)

                def body(carry):
                    t = carry[0]
                    still_alive = self.alive().astype(jnp.int32)
                    self.step(score_t=(t, 0), csum_t=(t - 1, 1), pv_t=(t - 2, 0))
                    self.step(score_t=(t + 1, 1), csum_t=(t, 0), pv_t=(t - 1, 1))
                    return t + 2, still_alive

                t_end, _ = lax.while_loop(cond, body, (jnp.int32(6), jnp.int32(1)))
                self.step(csum_t=(t_end - 1, 1), pv_t=(t_end - 2, 0))
                self.step(pv_t=(t_end - 1, 1))


class _Forgetting:
    def __init__(self, h, i, kpre_ref, cend_ref, qT_ref, k_ref, vT_ref, ct_ref, sets,
                 acc_ref, m_ref, l_ref, qn_ref):
        self.h, self.kpre_ref, self.cend_ref = h, kpre_ref, cend_ref
        self.qT_ref, self.k_ref, self.vT_ref, self.ct_ref = qT_ref, k_ref, vT_ref, ct_ref
        self.sets, self.acc_ref, self.m_ref, self.l_ref, self.qn_ref = (
            sets, acc_ref, m_ref, l_ref, qn_ref)
        self.n_tiles = NSUB * (i + 1)
        self.d0 = NSUB * i

    def key_tile(self, t):
        return self.n_tiles - 1 - t

    def init(self):
        self.acc_ref[...] = jnp.zeros_like(self.acc_ref)
        self.l_ref[...] = jnp.zeros_like(self.l_ref)
        self.m_ref[...] = jnp.full(self.m_ref.shape, -jnp.inf, F32)
        qf = self.qT_ref[0:HEAD_DIM, :].astype(F32)
        self.qn_ref[...] = jnp.sqrt(jnp.sum(qf * qf, axis=0, keepdims=True))

    def diag(self):
        return [(self.d0, 0, _key_before_query(BQ, strict=False)),
                (self.d0 + 1, BK, _key_before_query(BQ - BK, strict=False))]

    def scores(self, descs):
        return [_dot(_key_block(self.k_ref, j), self.qT_ref[:, q0:]) for j, q0, _ in descs]

    def soft_group(self, descs, us):
        us = [u if mask is None else jnp.where(mask, u, -jnp.inf)
              for (_, _, mask), u in zip(descs, us)]
        mx = None
        for (_, q0, _), u in zip(descs, us):
            cm = _widen(jnp.max(u, axis=0, keepdims=True), q0, -jnp.inf)
            mx = cm if mx is None else jnp.maximum(mx, cm)
        ct = self.ct_ref[...]
        m_old = self.m_ref[...]
        m_new = jnp.maximum(m_old, mx + ct)
        off = ct - m_new
        alpha = jnp.exp(m_old - m_new)
        l = alpha * self.l_ref[...]
        acc = alpha * self.acc_ref[...]
        for (j, q0, _), u in zip(descs, us):
            p = jnp.exp(u + off[:, q0:])
            l = l + _widen(jnp.sum(p, axis=0, keepdims=True), q0, 0.0)
            acc = acc + _widen(_dot(self.vT_ref[j], p.astype(BF16)), q0, 0.0)
        self.m_ref[...] = m_new
        self.l_ref[...] = l
        self.acc_ref[...] = acc

    def batch(self, groups):
        scores = [self.scores(descs) for descs in groups]
        for descs, us in zip(groups, scores):
            self.soft_group(descs, us)

    def step(self, score_t=None, soft_t=None, pv_t=None):
        if score_t is not None:
            u = _dot(_key_block(self.k_ref, score_t[0]), self.qT_ref[...])
        if pv_t is not None:
            pv_set = self.sets[pv_t[1]]
            self.acc_ref[...] = pv_set[3][...] * self.acc_ref[...] + _dot(
                self.vT_ref[pv_t[0]], pv_set[2][...])
        if score_t is not None:
            us_ref, mxs_ref, _, _ = self.sets[score_t[1]]
            us_ref[...] = u
            mxs_ref[...] = jnp.max(u, axis=0, keepdims=True)
        if soft_t is not None:
            u_ref, mx_ref, p_ref, al_ref = self.sets[soft_t[1]]
            ct = self.ct_ref[...]
            m_old = self.m_ref[...]
            m_new = jnp.maximum(m_old, mx_ref[...] + ct)
            p = jnp.exp(u_ref[...] + (ct - m_new))
            alpha = jnp.exp(m_old - m_new)
            self.m_ref[...] = m_new
            self.l_ref[...] = alpha * self.l_ref[...] + jnp.sum(p, axis=0, keepdims=True)
            p_ref[...] = p.astype(BF16)
            al_ref[...] = alpha

    def alive(self, t):
        j = jnp.maximum(self.key_tile(t), 0)
        kmax = self.kpre_ref[self.h, j]
        c_end = self.cend_ref[self.h, j]
        bound = (self.qn_ref[...] * (kmax * FX_NORM_SLACK)
                 + (self.ct_ref[...] - c_end) - self.m_ref[...])
        return jnp.max(bound) >= -(FX_DEAD_LN + 1e-5 * jnp.abs(c_end))

    def rest(self):
        key_tile = self.key_tile

        @pl.when(jnp.logical_and(self.alive(4), self.n_tiles > 4))
        def _():
            self.step(score_t=(key_tile(4), 0))
            self.step(score_t=(key_tile(5), 1), soft_t=(key_tile(4), 0))

            def cond(carry):
                return jnp.logical_and(carry[0] < self.n_tiles, carry[1] > 0)

            def body(carry):
                t = carry[0]
                next_alive = self.alive(t + 2).astype(jnp.int32)
                self.step(score_t=(key_tile(t), 0), soft_t=(key_tile(t - 1), 1),
                          pv_t=(key_tile(t - 2), 0))
                self.step(score_t=(key_tile(t + 1), 1), soft_t=(key_tile(t), 0),
                          pv_t=(key_tile(t - 1), 1))
                return t + 2, next_alive

            t_end, _ = lax.while_loop(
                cond, body, (jnp.int32(6), self.alive(6).astype(jnp.int32)))
            self.step(soft_t=(key_tile(t_end - 1), 1), pv_t=(key_tile(t_end - 2), 0))
            self.step(pv_t=(key_tile(t_end - 1), 1))


N_SB_SCRATCH = 12
N_FX_SCRATCH = 12


def _attn_kernel(kpre_ref, cend_ref, qs_ref, qf_ref, ks_ref, kf_ref, vs_ref, vf_ref,
                 tri_ref, ct_ref, gs_ref, gf_ref, os_ref, of_ref, *scratch):
    h = pl.program_id(0)
    i = pl.program_id(1)
    s_scr = scratch[:N_SB_SCRATCH]
    f_scr = scratch[N_SB_SCRATCH:]
    sb = _StickBreaking(i, qs_ref, ks_ref, vs_ref, tri_ref,
                        (s_scr[0:5], s_scr[5:10]), s_scr[10], s_scr[11])
    fx = _Forgetting(h, i, kpre_ref, cend_ref, qf_ref, kf_ref, vf_ref, ct_ref,
                     (f_scr[0:4], f_scr[4:8]), f_scr[8], f_scr[9], f_scr[10], f_scr[11])
    sb.init()
    fx.init()

    @pl.when(i == 0)
    def _():
        sb.batch(sb.diag())
        fx.batch([fx.diag()])

    @pl.when(i > 0)
    def _():
        s_descs = sb.diag() + [(2, 0, None)]
        f_groups = [fx.diag(), [(fx.key_tile(2), 0, None), (fx.key_tile(3), 0, None)]]
        s_z = sb.scores(s_descs)
        f_u = [fx.scores(descs) for descs in f_groups]
        parts = sb.splits(s_descs, s_z)
        csums = [sb.suffix_sums(*parts[0])]
        fx.soft_group(f_groups[0], f_u[0])
        csums += [sb.suffix_sums(hi, lo) for hi, lo in parts[1:]]
        fx.soft_group(f_groups[1], f_u[1])
        sb.finish(s_descs, s_z, csums)
        sb.rest()
        fx.rest()

    os_ref[...] = _head_rmsnorm_T(sb.acc_ref[...], gs_ref[...]).astype(os_ref.dtype)
    of_ref[...] = _head_rmsnorm_T(fx.acc_ref[...] / fx.l_ref[...],
                                  gf_ref[...]).astype(of_ref.dtype)


def _attn_scratch():
    tile_f32 = pltpu.VMEM((BK, BQ), F32)
    tile_bf16 = pltpu.VMEM((BK, BQ), BF16)
    row = pltpu.VMEM((1, BQ), F32)
    acc = pltpu.VMEM((HEAD_DIM, BQ), F32)
    sb_set = [tile_f32, tile_bf16, tile_bf16, tile_bf16, row]
    fx_set = [tile_f32, row, tile_bf16, row]
    sb = sb_set + sb_set + [acc, row]
    fx = fx_set + fx_set + [acc, row, row, row]
    assert len(sb) == N_SB_SCRATCH and len(fx) == N_FX_SCRATCH
    return sb + fx


def _attention(qT, k, vT, tri, ct, kpre, cend, g_sb_col, g_fx_col):
    s = k.shape[0]
    nq = s // BQ
    nkb = s // BK
    smem_spec = pl.BlockSpec(memory_space=pltpu.SMEM)
    out_spec = pl.BlockSpec((HEAD_DIM, BQ), lambda h, i: (h, i))
    out_shape = jax.ShapeDtypeStruct((N_SB * HEAD_DIM, s), BF16)
    return pl.pallas_call(
        _attn_kernel,
        grid=(N_SB, nq),
        in_specs=[
            smem_spec, smem_spec,
            pl.BlockSpec((HEAD_PAD, BQ), lambda h, i: (h, i)),
            pl.BlockSpec((HEAD_PAD, BQ), lambda h, i: (N_SB + h, i)),
            pl.BlockSpec((s, HEAD_PAD), lambda h, i: (0, h)),
            pl.BlockSpec((s, HEAD_PAD), lambda h, i: (0, N_SB + h)),
            pl.BlockSpec((nkb, HEAD_DIM, BK), lambda h, i: (0, h, 0)),
            pl.BlockSpec((nkb, HEAD_DIM, BK), lambda h, i: (0, N_SB + h, 0)),
            pl.BlockSpec((BK, BK), lambda h, i: (0, 0)),
            pl.BlockSpec((None, 1, BQ), lambda h, i: (h, 0, i)),
            pl.BlockSpec((HEAD_DIM, 1), lambda h, i: (h, 0)),
            pl.BlockSpec((HEAD_DIM, 1), lambda h, i: (h, 0)),
        ],
        out_specs=[out_spec, out_spec],
        out_shape=[out_shape, out_shape],
        scratch_shapes=_attn_scratch(),
        compiler_params=pltpu.CompilerParams(
            dimension_semantics=("arbitrary", "arbitrary"),
            vmem_limit_bytes=VMEM_LIMIT),
        name="attn",
    )(kpre, cend, qT, qT, k, k, vT, vT, tri, ct, g_sb_col, g_fx_col)


def _layer_norm(v, g, b):
    mu = jnp.mean(v, axis=-1, keepdims=True)
    vc = v - mu
    var = jnp.mean(vc * vc, axis=-1, keepdims=True)
    return vc * lax.rsqrt(var + LN_EPS) * g + b


def _ffn_kernel(alpha, x_ref, osb_ref, ofx_ref, woa_ref, wob_ref,
                g1_ref, b1_ref, g2_ref, b2_ref, wgu_ref, wd_ref,
                y_ref, act_ref):
    d_ff = wd_ref.shape[0]
    mix = (lax.dot_general(osb_ref[...], woa_ref[...], _TN, preferred_element_type=F32)
           + lax.dot_general(ofx_ref[...], wob_ref[...], _TN, preferred_element_type=F32))
    h1 = _layer_norm(alpha * x_ref[...] + mix, g1_ref[...], b1_ref[...])
    h1b = h1.astype(BF16)
    for c in range(d_ff // FF_CHUNK):
        lo, hi = c * FF_CHUNK, (c + 1) * FF_CHUNK
        gate = jnp.dot(h1b, wgu_ref[:, lo:hi], preferred_element_type=F32)
        up = jnp.dot(h1b, wgu_ref[:, d_ff + lo:d_ff + hi], preferred_element_type=F32)
        act_ref[:, lo:hi] = (gate * jax.nn.sigmoid(gate) * up).astype(BF16)
    ff = jnp.dot(act_ref[...], wd_ref[...], preferred_element_type=F32)
    y_ref[...] = _layer_norm(alpha * h1 + ff, g2_ref[...], b2_ref[...])


def _out_ffn(alpha, x2, oT_sb, oT_fx, wo_a, wo_b, g1, b1, g2, b2, wgu, wd):
    s, dm = x2.shape
    bs = BLOCK_ROWS
    d_ff = wd.shape[0]
    half = oT_sb.shape[0]
    return pl.pallas_call(
        functools.partial(_ffn_kernel, alpha),
        grid=(s // bs,),
        in_specs=[
            pl.BlockSpec((bs, dm), lambda i: (i, 0)),
            pl.BlockSpec((half, bs), lambda i: (0, i)),
            pl.BlockSpec((half, bs), lambda i: (0, i)),
            _const_spec(wo_a.shape),
            _const_spec(wo_b.shape),
            _const_spec(g1.shape),
            _const_spec(b1.shape),
            _const_spec(g2.shape),
            _const_spec(b2.shape),
            _const_spec(wgu.shape),
            _const_spec(wd.shape),
        ],
        out_specs=pl.BlockSpec((bs, dm), lambda i: (i, 0)),
        out_shape=jax.ShapeDtypeStruct((s, dm), F32),
        scratch_shapes=[pltpu.VMEM((bs, d_ff), BF16)],
        compiler_params=pltpu.CompilerParams(
            dimension_semantics=("arbitrary",), vmem_limit_bytes=VMEM_LIMIT),
        name="out_ffn",
    )(x2, oT_sb, oT_fx, wo_a, wo_b, g1, b1, g2, b2, wgu, wd)


def _layer(x2, w_in, b_f, g_sb, g_fox, w_out, ln1_g, ln1_b, ln2_g, ln2_b,
           w_gate_up, w_down, alpha):
    assert NSUB == 2
    s, dm = x2.shape
    sbw = N_SB * HEAD_DIM
    fxw = N_FX * HEAD_DIM
    q_sb, k_sb, v_sb = w_in[:, :sbw], w_in[:, sbw:2 * sbw], w_in[:, 2 * sbw:3 * sbw]
    o = 3 * sbw
    q_fx, k_fx, v_fx = w_in[:, o:o + fxw], w_in[:, o + fxw:o + 2 * fxw], w_in[:, o + 2 * fxw:o + 3 * fxw]
    wf = w_in[:, o + 3 * fxw:]

    wqT = jnp.concatenate([q_sb, q_fx], axis=1).T.astype(BF16)
    wk = jnp.concatenate([k_sb, k_fx], axis=1).reshape(dm, N_HEADS, HEAD_DIM)
    wk = jnp.pad(wk, ((0, 0), (0, 0), (0, HEAD_PAD - HEAD_DIM)))
    wk = wk.reshape(dm, N_HEADS * HEAD_PAD).astype(BF16)
    wvT = jnp.concatenate([v_sb, v_fx], axis=1).T.astype(BF16)

    wf_hi = wf.astype(BF16)
    wf_lo = (wf - wf_hi.astype(F32)).astype(BF16)
    wf2 = jnp.concatenate([wf_hi, wf_lo], axis=1)

    qT, k, vT, c, kpre, cend = _projection(x2, wqT, wk, wvT, wf2, b_f.reshape(1, N_FX))

    srow = lax.broadcasted_iota(jnp.int32, (BK, BK), 0)
    jcol = lax.broadcasted_iota(jnp.int32, (BK, BK), 1)
    tri = (jcol >= srow).astype(BF16)
    ct = c.T.reshape(N_FX, 1, s)
    kpre = kpre.reshape(s // BK, N_FX).T
    cend = cend.reshape(s // BK, N_FX).T

    oT_sb, oT_fx = _attention(qT, k, vT, tri, ct, kpre, cend,
                              g_sb.reshape(sbw, 1), g_fox.reshape(fxw, 1))

    wo = w_out.astype(BF16)
    return _out_ffn(
        alpha, x2, oT_sb, oT_fx, wo[:sbw], wo[sbw:],
        ln1_g.reshape(1, dm), ln1_b.reshape(1, dm),
        ln2_g.reshape(1, dm), ln2_b.reshape(1, dm),
        w_gate_up.astype(BF16), w_down.astype(BF16))


def kernel(x, w_in, b_f, g_sb, g_fox, w_out, ln1_g, ln1_b, ln2_g, ln2_b, w_gate_up, w_down):
    batch, s, dm = x.shape
    depth = w_in.shape[0]
    alpha = (2 * depth) ** 0.25
    outs = []
    for b in range(batch):
        h = x[b]
        for l in range(depth):
            h = _layer(h, w_in[l], b_f[l], g_sb[l], g_fox[l], w_out[l],
                       ln1_g[l], ln1_b[l], ln2_g[l], ln2_b[l],
                       w_gate_up[l], w_down[l], alpha)
        outs.append(h)
    return outs[0][None] if batch == 1 else jnp.stack(outs, axis=0)
```

```python
import functools

import jax
import jax.numpy as jnp
from jax import lax
from jax.experimental import pallas as pl
from jax.experimental.pallas import tpu as pltpu

F32 = jnp.float32
BF16 = jnp.bfloat16

HEAD_DIM = 64
HEAD_PAD = 128
N_SB = 8
N_FX = 8
N_HEADS = N_SB + N_FX
LN_EPS = 1e-5
RMS_EPS = 1e-6
LOG2E = 1.4426950408889634

BLOCK_ROWS = 512
BQ = 512
BK = 256
NSUB = BQ // BK
FF_CHUNK = 256
VMEM_LIMIT = 56 * 1024 * 1024

SB_DEAD_LOG2 = 160.0
FX_DEAD_LN = 110.0
FX_NORM_SLACK = 1.001

_NT = (((1,), (1,)), ((), ()))
_TN = (((0,), (0,)), ((), ()))


def _const_spec(shape):
    nd = len(shape)
    return pl.BlockSpec(shape, lambda *_: (0,) * nd, pipeline_mode=pl.Buffered(1))


def _proj_kernel(x_ref, wqT_ref, wk_ref, wvT_ref, wf_ref, bf_ref,
                 qT_ref, k_ref, vT_ref, c_ref, kpre_ref, cend_ref,
                 carry_ref, kmax_ref):
    i = pl.program_id(0)
    bs = x_ref.shape[0]
    n_kt = bs // BK

    @pl.when(i == 0)
    def _():
        carry_ref[...] = jnp.zeros_like(carry_ref)
        kmax_ref[...] = jnp.zeros_like(kmax_ref)

    xb = x_ref[...]
    xbf = xb.astype(BF16)

    x_lo = (xb - xbf.astype(F32)).astype(BF16)
    wf2 = wf_ref[...]
    f_hi = jnp.dot(xbf, wf2, preferred_element_type=F32)
    f_lo = jnp.dot(x_lo, wf2[:, :N_FX], preferred_element_type=F32)
    f = f_hi[:, :N_FX] + f_hi[:, N_FX:] + f_lo + bf_ref[...]
    logf = jnp.minimum(f, 0.0) - jnp.log1p(jnp.exp(-jnp.abs(f)))
    row = lax.broadcasted_iota(jnp.int32, logf.shape, 0)
    c = logf
    d = 1
    while d < bs:
        c = c + jnp.where(row >= d, pltpu.roll(c, d, axis=0), 0.0)
        d *= 2
    c = c + carry_ref[...]
    carry_ref[...] = c[bs - 1:bs, :]
    c_ref[...] = c
    stat_tile = lax.broadcasted_iota(jnp.int32, (n_kt, N_FX), 0)
    stat_head = lax.broadcasted_iota(jnp.int32, (n_kt, N_FX), 1)
    c_end = jnp.zeros((n_kt, N_FX), F32)
    for t in range(n_kt):
        last = jnp.broadcast_to(c[(t + 1) * BK - 1:(t + 1) * BK, :], (n_kt, N_FX))
        c_end = jnp.where(stat_tile == t, last, c_end)
    cend_ref[0] = c_end

    lane = lax.broadcasted_iota(jnp.int32, (bs, HEAD_PAD), 1)
    knorm2 = jnp.zeros((n_kt, N_FX), F32)
    for hp in range(N_HEADS // 2):
        r = jnp.dot(xbf, wk_ref[:, hp * 256:(hp + 1) * 256],
                    preferred_element_type=F32)
        for sub in range(2):
            h = 2 * hp + sub
            rh = r[:, sub * HEAD_PAD:(sub + 1) * HEAD_PAD]
            if h >= N_SB:
                hh = h - N_SB
                kr = rh.astype(BF16).astype(F32)
                n2 = jnp.sum(kr * kr, axis=1, keepdims=True)
                for t in range(n_kt):
                    tmax = jnp.max(n2[t * BK:(t + 1) * BK, :], axis=0, keepdims=True)
                    knorm2 = jnp.where((stat_tile == t) & (stat_head == hh),
                                       jnp.broadcast_to(tmax, (n_kt, N_FX)), knorm2)
                cb = jnp.broadcast_to(c[:, hh:hh + 1], (bs, HEAD_PAD))
                hi = cb.astype(BF16).astype(F32)
                r1 = cb - hi
                mid = r1.astype(BF16).astype(F32)
                lo = (r1 - mid).astype(BF16).astype(F32)
                aug = jnp.where(lane == HEAD_DIM, hi,
                                jnp.where(lane == HEAD_DIM + 1, mid,
                                          jnp.where(lane == HEAD_DIM + 2, lo, 0.0)))
                rh = rh + aug
            k_ref[:, h * HEAD_PAD:(h + 1) * HEAD_PAD] = rh.astype(BF16)

    knorm = jnp.sqrt(knorm2)
    k_pre = jnp.zeros((n_kt, N_FX), F32)
    run_max = kmax_ref[...]
    for t in range(n_kt):
        run_max = jnp.maximum(run_max, knorm[t:t + 1, :])
        k_pre = jnp.where(stat_tile == t, jnp.broadcast_to(run_max, (n_kt, N_FX)), k_pre)
    kmax_ref[...] = run_max
    kpre_ref[0] = k_pre

    pad_rows = HEAD_PAD - HEAD_DIM
    prow = lax.broadcasted_iota(jnp.int32, (pad_rows, bs), 0)
    neg_rows = jnp.where(prow < 3, -1.0, 0.0).astype(BF16)
    zero_rows = jnp.zeros((pad_rows, bs), BF16)
    for cidx in range(4):
        r = lax.dot_general(wqT_ref[cidx * 256:(cidx + 1) * 256, :], xbf, _NT,
                            preferred_element_type=F32)
        for sub in range(4):
            h = cidx * 4 + sub
            base = h * HEAD_PAD
            scale = HEAD_DIM ** -0.5 * (LOG2E if h < N_SB else 1.0)
            qT_ref[base:base + HEAD_DIM, :] = (
                r[sub * HEAD_DIM:(sub + 1) * HEAD_DIM, :] * scale).astype(BF16)
            qT_ref[base + HEAD_DIM:base + HEAD_PAD, :] = (
                neg_rows if h >= N_SB else zero_rows)

    for cidx in range(4):
        r = lax.dot_general(wvT_ref[cidx * 256:(cidx + 1) * 256, :], xbf, _NT,
                            preferred_element_type=F32)
        for jb in range(n_kt):
            vT_ref[jb, cidx * 256:(cidx + 1) * 256, :] = (
                r[:, jb * BK:(jb + 1) * BK].astype(BF16))


def _projection(x2, wqT, wk, wvT, wf, bf):
    s, dm = x2.shape
    bs = BLOCK_ROWS
    n_kt = bs // BK
    return pl.pallas_call(
        _proj_kernel,
        grid=(s // bs,),
        in_specs=[
            pl.BlockSpec((bs, dm), lambda i: (i, 0)),
            _const_spec(wqT.shape),
            _const_spec(wk.shape),
            _const_spec(wvT.shape),
            _const_spec(wf.shape),
            _const_spec(bf.shape),
        ],
        out_specs=[
            pl.BlockSpec((N_HEADS * HEAD_PAD, bs), lambda i: (0, i)),
            pl.BlockSpec((bs, N_HEADS * HEAD_PAD), lambda i: (i, 0)),
            pl.BlockSpec((n_kt, N_HEADS * HEAD_DIM, BK), lambda i: (i, 0, 0)),
            pl.BlockSpec((bs, N_FX), lambda i: (i, 0)),
            pl.BlockSpec((1, n_kt, N_FX), lambda i: (i, 0, 0)),
            pl.BlockSpec((1, n_kt, N_FX), lambda i: (i, 0, 0)),
        ],
        out_shape=[
            jax.ShapeDtypeStruct((N_HEADS * HEAD_PAD, s), BF16),
            jax.ShapeDtypeStruct((s, N_HEADS * HEAD_PAD), BF16),
            jax.ShapeDtypeStruct((s // BK, N_HEADS * HEAD_DIM, BK), BF16),
            jax.ShapeDtypeStruct((s, N_FX), F32),
            jax.ShapeDtypeStruct((s // bs, n_kt, N_FX), F32),
            jax.ShapeDtypeStruct((s // bs, n_kt, N_FX), F32),
        ],
        scratch_shapes=[pltpu.VMEM((1, N_FX), F32), pltpu.VMEM((1, N_FX), F32)],
        compiler_params=pltpu.CompilerParams(
            dimension_semantics=("arbitrary",), vmem_limit_bytes=VMEM_LIMIT),
        name="proj",
    )(x2, wqT, wk, wvT, wf, bf)


def _head_rmsnorm_T(oT, g_col):
    ms = jnp.mean(oT * oT, axis=0, keepdims=True)
    return oT * lax.rsqrt(ms + RMS_EPS) * g_col


def _key_before_query(n_queries, strict):
    srow = lax.broadcasted_iota(jnp.int32, (BK, n_queries), 0)
    tcol = lax.broadcasted_iota(jnp.int32, (BK, n_queries), 1)
    return srow < tcol if strict else srow <= tcol


def _key_block(k_ref, j):
    return k_ref[pl.ds(pl.multiple_of(j * BK, BK), BK), :]


def _dot(a, b):
    return jnp.dot(a, b, preferred_element_type=F32)


def _place(x, q_lo, q_hi, fill):
    pieces = []
    if q_lo:
        pieces.append(jnp.full((x.shape[0], q_lo), fill, F32))
    pieces.append(x)
    if q_hi < BQ:
        pieces.append(jnp.full((x.shape[0], BQ - q_hi), fill, F32))
    return pieces[0] if len(pieces) == 1 else jnp.concatenate(pieces, axis=1)


def _widen(x, q0, fill):
    return _place(x, q0, BQ, fill)


class _StickBreaking:
    def __init__(self, i, qT_ref, k_ref, vT_ref, tri_ref, sets, acc_ref, run_ref):
        self.qT_ref, self.k_ref, self.vT_ref, self.tri_ref = qT_ref, k_ref, vT_ref, tri_ref
        self.sets, self.acc_ref, self.run_ref = sets, acc_ref, run_ref
        self.n_tiles = NSUB * (i + 1)

    def key_tile(self, t):
        return self.n_tiles - 1 - t

    def init(self):
        self.acc_ref[...] = jnp.zeros_like(self.acc_ref)
        self.run_ref[...] = jnp.zeros_like(self.run_ref)

    def diag(self):
        return [(0, BK, BQ, _key_before_query(BQ - BK, strict=True)),
                (1, 0, BQ, _key_before_query(BQ, strict=True))]

    @staticmethod
    def softplus2(z):
        return jnp.maximum(z, 0.0) + jnp.log(1.0 + jnp.exp2(-jnp.abs(z))) * LOG2E

    @staticmethod
    def split(sp):
        hi = sp.astype(BF16)
        return hi, (sp - hi.astype(F32)).astype(BF16)

    def suffix_sums(self, hi, lo):
        tri = self.tri_ref[...]
        return _dot(tri, hi) + _dot(tri, lo)

    def scores(self, descs):
        return [_dot(_key_block(self.k_ref, self.key_tile(t)), self.qT_ref[:, q_lo:q_hi])
                for t, q_lo, q_hi, _ in descs]

    def splits(self, descs, zs):
        parts = []
        for desc, z in zip(descs, zs):
            sp = self.softplus2(z)
            if desc[3] is not None:
                sp = jnp.where(desc[3], sp, 0.0)
            parts.append(self.split(sp))
        return parts

    def finish(self, descs, zs, csums):
        ws = []
        for desc, z, csum in zip(descs, zs, csums):
            w = jnp.exp2(z - csum)
            if desc[3] is not None:
                w = jnp.where(desc[3], w, 0.0)
            ws.append(w.astype(BF16))
        pvs = [_dot(self.vT_ref[self.key_tile(desc[0])], w) for desc, w in zip(descs, ws)]
        run = self.run_ref[...]
        acc = self.acc_ref[...]
        for (_, q_lo, q_hi, _), csum, pv in zip(descs, csums, pvs):
            acc = acc + _place(pv, q_lo, q_hi, 0.0) * jnp.exp2(-run)
            run = run + _place(csum[0:1, :], q_lo, q_hi, 0.0)
        self.acc_ref[...] = acc
        self.run_ref[...] = run

    def batch(self, descs):
        zs = self.scores(descs)
        parts = self.splits(descs, zs)
        self.finish(descs, zs, [self.suffix_sums(hi, lo) for hi, lo in parts])

    def step(self, score_t=None, csum_t=None, pv_t=None):
        if score_t is not None:
            z = _dot(_key_block(self.k_ref, self.key_tile(score_t[0])), self.qT_ref[...])
        if csum_t is not None:
            z_ref, hi_ref, lo_ref, w_ref, r_ref = self.sets[csum_t[1]]
            csum = self.suffix_sums(hi_ref[...], lo_ref[...])
        if pv_t is not None:
            pv_set = self.sets[pv_t[1]]
            self.acc_ref[...] += pv_set[4][...] * _dot(
                self.vT_ref[self.key_tile(pv_t[0])], pv_set[3][...])
        if score_t is not None:
            zs_ref, his_ref, los_ref, _, _ = self.sets[score_t[1]]
            hi, lo = self.split(self.softplus2(z))
            zs_ref[...] = z
            his_ref[...] = hi
            los_ref[...] = lo
        if csum_t is not None:
            w_ref[...] = jnp.exp2(z_ref[...] - csum).astype(BF16)
            run = self.run_ref[...]
            r_ref[...] = jnp.exp2(-run)
            self.run_ref[...] = run + csum[0:1, :]

    def alive_in(self, q_lo, q_hi):
        return jnp.min(self.run_ref[:, q_lo:q_hi]) < SB_DEAD_LOG2

    def alive(self):
        return self.alive_in(0, BQ)

    def rest(self, enter):
        @pl.when(enter)
        def _():
            self._rest_if_alive()

    def _rest_if_alive(self):
        @pl.when(self.alive())
        def _():
            self.batch([(3, 0, BQ, None)])

            @pl.when(jnp.logical_and(self.alive(), self.n_tiles > 4))
            def _():
                self.step(score_t=(4, 0))
                self.step(score_t=(5, 1), csum_t=(4, 0))

                def cond(carry):
                    return jnp.logical_and(carry[0] < self.n_tiles, carry[1] > 0)

                def body(carry):
                    t = carry[0]
                    still_alive = self.alive().astype(jnp.int32)
                    self.step(score_t=(t, 0), csum_t=(t - 1, 1), pv_t=(t - 2, 0))
                    self.step(score_t=(t + 1, 1), csum_t=(t, 0), pv_t=(t - 1, 1))
                    return t + 2, still_alive

                t_end, _ = lax.while_loop(cond, body, (jnp.int32(6), jnp.int32(1)))
                self.step(csum_t=(t_end - 1, 1), pv_t=(t_end - 2, 0))
                self.step(pv_t=(t_end - 1, 1))


class _Forgetting:
    def __init__(self, h, i, kpre_ref, cend_ref, qT_ref, k_ref, vT_ref, ct_ref, sets,
                 acc_ref, m_ref, l_ref, qn_ref):
        self.h, self.kpre_ref, self.cend_ref = h, kpre_ref, cend_ref
        self.qT_ref, self.k_ref, self.vT_ref, self.ct_ref = qT_ref, k_ref, vT_ref, ct_ref
        self.sets, self.acc_ref, self.m_ref, self.l_ref, self.qn_ref = (
            sets, acc_ref, m_ref, l_ref, qn_ref)
        self.n_tiles = NSUB * (i + 1)
        self.d0 = NSUB * i

    def key_tile(self, t):
        return self.n_tiles - 1 - t

    def init(self):
        self.acc_ref[...] = jnp.zeros_like(self.acc_ref)
        self.l_ref[...] = jnp.zeros_like(self.l_ref)
        self.m_ref[...] = jnp.full(self.m_ref.shape, -jnp.inf, F32)

    def set_query_norms(self):
        qf = self.qT_ref[0:HEAD_DIM, :].astype(F32)
        self.qn_ref[...] = jnp.sqrt(jnp.sum(qf * qf, axis=0, keepdims=True))

    def diag(self):
        return [(self.d0, 0, _key_before_query(BQ, strict=False)),
                (self.d0 + 1, BK, _key_before_query(BQ - BK, strict=False))]

    def scores(self, descs):
        return [_dot(_key_block(self.k_ref, j), self.qT_ref[:, q0:]) for j, q0, _ in descs]

    def soft_group(self, descs, us):
        us = [u if mask is None else jnp.where(mask, u, -jnp.inf)
              for (_, _, mask), u in zip(descs, us)]
        mx = None
        for (_, q0, _), u in zip(descs, us):
            cm = _widen(jnp.max(u, axis=0, keepdims=True), q0, -jnp.inf)
            mx = cm if mx is None else jnp.maximum(mx, cm)
        ct = self.ct_ref[...]
        m_old = self.m_ref[...]
        m_new = jnp.maximum(m_old, mx + ct)
        off = ct - m_new
        alpha = jnp.exp(m_old - m_new)
        l = alpha * self.l_ref[...]
        acc = alpha * self.acc_ref[...]
        for (j, q0, _), u in zip(descs, us):
            p = jnp.exp(u + off[:, q0:])
            l = l + _widen(jnp.sum(p, axis=0, keepdims=True), q0, 0.0)
            acc = acc + _widen(_dot(self.vT_ref[j], p.astype(BF16)), q0, 0.0)
        self.m_ref[...] = m_new
        self.l_ref[...] = l
        self.acc_ref[...] = acc

    def batch(self, groups):
        scores = [self.scores(descs) for descs in groups]
        for descs, us in zip(groups, scores):
            self.soft_group(descs, us)

    def step(self, score_t=None, soft_t=None, pv_t=None):
        if score_t is not None:
            u = _dot(_key_block(self.k_ref, score_t[0]), self.qT_ref[...])
        if pv_t is not None:
            pv_set = self.sets[pv_t[1]]
            self.acc_ref[...] = pv_set[3][...] * self.acc_ref[...] + _dot(
                self.vT_ref[pv_t[0]], pv_set[2][...])
        if score_t is not None:
            us_ref, mxs_ref, _, _ = self.sets[score_t[1]]
            us_ref[...] = u
            mxs_ref[...] = jnp.max(u, axis=0, keepdims=True)
        if soft_t is not None:
            u_ref, mx_ref, p_ref, al_ref = self.sets[soft_t[1]]
            ct = self.ct_ref[...]
            m_old = self.m_ref[...]
            m_new = jnp.maximum(m_old, mx_ref[...] + ct)
            p = jnp.exp(u_ref[...] + (ct - m_new))
            alpha = jnp.exp(m_old - m_new)
            self.m_ref[...] = m_new
            self.l_ref[...] = alpha * self.l_ref[...] + jnp.sum(p, axis=0, keepdims=True)
            p_ref[...] = p.astype(BF16)
            al_ref[...] = alpha

    def alive(self, t):
        j = jnp.maximum(self.key_tile(t), 0)
        kmax = self.kpre_ref[self.h, j]
        c_end = self.cend_ref[self.h, j]
        bound = (self.qn_ref[...] * (kmax * FX_NORM_SLACK)
                 + (self.ct_ref[...] - c_end) - self.m_ref[...])
        return jnp.max(bound) >= -(FX_DEAD_LN + 1e-5 * jnp.abs(c_end))

    def more_after_four(self):
        return jnp.logical_and(self.alive(4), self.n_tiles > 4)

    def rest(self, enter):
        key_tile = self.key_tile

        @pl.when(enter)
        def _():
            self.step(score_t=(key_tile(4), 0))
            self.step(score_t=(key_tile(5), 1), soft_t=(key_tile(4), 0))

            def cond(carry):
                return jnp.logical_and(carry[0] < self.n_tiles, carry[1] > 0)

            def body(carry):
                t = carry[0]
                next_alive = self.alive(t + 2).astype(jnp.int32)
                self.step(score_t=(key_tile(t), 0), soft_t=(key_tile(t - 1), 1),
                          pv_t=(key_tile(t - 2), 0))
                self.step(score_t=(key_tile(t + 1), 1), soft_t=(key_tile(t), 0),
                          pv_t=(key_tile(t - 1), 1))
                return t + 2, next_alive

            t_end, _ = lax.while_loop(
                cond, body, (jnp.int32(6), self.alive(6).astype(jnp.int32)))
            self.step(soft_t=(key_tile(t_end - 1), 1), pv_t=(key_tile(t_end - 2), 0))
            self.step(pv_t=(key_tile(t_end - 1), 1))


N_SB_SCRATCH = 12
N_FX_SCRATCH = 12


def _attn_kernel(kpre_ref, cend_ref, qs_ref, qf_ref, ks_ref, kf_ref, vs_ref, vf_ref,
                 tri_ref, ct_ref, gs_ref, gf_ref, os_ref, of_ref, *scratch):
    h = pl.program_id(0)
    i = pl.program_id(1)
    s_scr = scratch[:N_SB_SCRATCH]
    f_scr = scratch[N_SB_SCRATCH:]
    sb = _StickBreaking(i, qs_ref, ks_ref, vs_ref, tri_ref,
                        (s_scr[0:5], s_scr[5:10]), s_scr[10], s_scr[11])
    fx = _Forgetting(h, i, kpre_ref, cend_ref, qf_ref, kf_ref, vf_ref, ct_ref,
                     (f_scr[0:4], f_scr[4:8]), f_scr[8], f_scr[9], f_scr[10], f_scr[11])
    sb.init()
    fx.init()

    @pl.when(i == 0)
    def _():
        sb.batch(sb.diag())
        fx.batch([fx.diag()])

    @pl.when(i > 0)
    def _():
        fx.set_query_norms()
        s_descs = sb.diag() + [(2, 0, BK, None)]
        f_groups = [fx.diag(), [(fx.key_tile(2), 0, None), (fx.key_tile(3), 0, None)]]
        s_z = sb.scores(s_descs)
        f_u = [fx.scores(descs) for descs in f_groups]
        parts = sb.splits(s_descs, s_z)
        csums = [sb.suffix_sums(*parts[0])]
        fx.soft_group(f_groups[0], f_u[0])
        csums += [sb.suffix_sums(hi, lo) for hi, lo in parts[1:]]
        fx.soft_group(f_groups[1], f_u[1])
        sb.finish(s_descs, s_z, csums)

        late_alive = sb.alive_in(BK, BQ)
        early_alive = sb.alive_in(0, BK)
        fx_more = fx.more_after_four()

        @pl.when(late_alive)
        def _():
            sb.batch([(2, BK, BQ, None)])

        sb.rest(jnp.logical_or(late_alive, early_alive))
        fx.rest(fx_more)

    os_ref[...] = _head_rmsnorm_T(sb.acc_ref[...], gs_ref[...]).astype(os_ref.dtype)
    of_ref[...] = _head_rmsnorm_T(fx.acc_ref[...] / fx.l_ref[...],
                                  gf_ref[...]).astype(of_ref.dtype)


def _attn_scratch():
    tile_f32 = pltpu.VMEM((BK, BQ), F32)
    tile_bf16 = pltpu.VMEM((BK, BQ), BF16)
    row = pltpu.VMEM((1, BQ), F32)
    acc = pltpu.VMEM((HEAD_DIM, BQ), F32)
    sb_set = [tile_f32, tile_bf16, tile_bf16, tile_bf16, row]
    fx_set = [tile_f32, row, tile_bf16, row]
    sb = sb_set + sb_set + [acc, row]
    fx = fx_set + fx_set + [acc, row, row, row]
    assert len(sb) == N_SB_SCRATCH and len(fx) == N_FX_SCRATCH
    return sb + fx


def _attention(qT, k, vT, tri, ct, kpre, cend, g_sb_col, g_fx_col):
    s = k.shape[0]
    nq = s // BQ
    nkb = s // BK
    smem_spec = pl.BlockSpec(memory_space=pltpu.SMEM)
    out_spec = pl.BlockSpec((HEAD_DIM, BQ), lambda h, i: (h, i))
    out_shape = jax.ShapeDtypeStruct((N_SB * HEAD_DIM, s), BF16)
    return pl.pallas_call(
        _attn_kernel,
        grid=(N_SB, nq),
        in_specs=[
            smem_spec, smem_spec,
            pl.BlockSpec((HEAD_PAD, BQ), lambda h, i: (h, i)),
            pl.BlockSpec((HEAD_PAD, BQ), lambda h, i: (N_SB + h, i)),
            pl.BlockSpec((s, HEAD_PAD), lambda h, i: (0, h)),
            pl.BlockSpec((s, HEAD_PAD), lambda h, i: (0, N_SB + h)),
            pl.BlockSpec((nkb, HEAD_DIM, BK), lambda h, i: (0, h, 0)),
            pl.BlockSpec((nkb, HEAD_DIM, BK), lambda h, i: (0, N_SB + h, 0)),
            pl.BlockSpec((BK, BK), lambda h, i: (0, 0)),
            pl.BlockSpec((None, 1, BQ), lambda h, i: (h, 0, i)),
            pl.BlockSpec((HEAD_DIM, 1), lambda h, i: (h, 0)),
            pl.BlockSpec((HEAD_DIM, 1), lambda h, i: (h, 0)),
        ],
        out_specs=[out_spec, out_spec],
        out_shape=[out_shape, out_shape],
        scratch_shapes=_attn_scratch(),
        compiler_params=pltpu.CompilerParams(
            dimension_semantics=("arbitrary", "arbitrary"),
            vmem_limit_bytes=VMEM_LIMIT),
        name="attn",
    )(kpre, cend, qT, qT, k, k, vT, vT, tri, ct, g_sb_col, g_fx_col)


def _layer_norm(v, g, b):
    mu = jnp.mean(v, axis=-1, keepdims=True)
    vc = v - mu
    var = jnp.mean(vc * vc, axis=-1, keepdims=True)
    return vc * lax.rsqrt(var + LN_EPS) * g + b


def _ffn_kernel(alpha, x_ref, osb_ref, ofx_ref, woa_ref, wob_ref,
                g1_ref, b1_ref, g2_ref, b2_ref, wgu_ref, wd_ref,
                y_ref, act_ref):
    d_ff = wd_ref.shape[0]
    mix = (lax.dot_general(osb_ref[...], woa_ref[...], _TN, preferred_element_type=F32)
           + lax.dot_general(ofx_ref[...], wob_ref[...], _TN, preferred_element_type=F32))
    h1 = _layer_norm(alpha * x_ref[...] + mix, g1_ref[...], b1_ref[...])
    h1b = h1.astype(BF16)
    for c in range(d_ff // FF_CHUNK):
        lo, hi = c * FF_CHUNK, (c + 1) * FF_CHUNK
        gate = jnp.dot(h1b, wgu_ref[:, lo:hi], preferred_element_type=F32)
        up = jnp.dot(h1b, wgu_ref[:, d_ff + lo:d_ff + hi], preferred_element_type=F32)
        act_ref[:, lo:hi] = (gate * jax.nn.sigmoid(gate) * up).astype(BF16)
    ff = jnp.dot(act_ref[...], wd_ref[...], preferred_element_type=F32)
    y_ref[...] = _layer_norm(alpha * h1 + ff, g2_ref[...], b2_ref[...])


def _out_ffn(alpha, x2, oT_sb, oT_fx, wo_a, wo_b, g1, b1, g2, b2, wgu, wd):
    s, dm = x2.shape
    bs = BLOCK_ROWS
    d_ff = wd.shape[0]
    half = oT_sb.shape[0]
    return pl.pallas_call(
        functools.partial(_ffn_kernel, alpha),
        grid=(s // bs,),
        in_specs=[
            pl.BlockSpec((bs, dm), lambda i: (i, 0)),
            pl.BlockSpec((half, bs), lambda i: (0, i)),
            pl.BlockSpec((half, bs), lambda i: (0, i)),
            _const_spec(wo_a.shape),
            _const_spec(wo_b.shape),
            _const_spec(g1.shape),
            _const_spec(b1.shape),
            _const_spec(g2.shape),
            _const_spec(b2.shape),
            _const_spec(wgu.shape),
            _const_spec(wd.shape),
        ],
        out_specs=pl.BlockSpec((bs, dm), lambda i: (i, 0)),
        out_shape=jax.ShapeDtypeStruct((s, dm), F32),
        scratch_shapes=[pltpu.VMEM((bs, d_ff), BF16)],
        compiler_params=pltpu.CompilerParams(
            dimension_semantics=("arbitrary",), vmem_limit_bytes=VMEM_LIMIT),
        name="out_ffn",
    )(x2, oT_sb, oT_fx, wo_a, wo_b, g1, b1, g2, b2, wgu, wd)


def _layer(x2, w_in, b_f, g_sb, g_fox, w_out, ln1_g, ln1_b, ln2_g, ln2_b,
           w_gate_up, w_down, alpha):
    assert NSUB == 2
    s, dm = x2.shape
    sbw = N_SB * HEAD_DIM
    fxw = N_FX * HEAD_DIM
    q_sb, k_sb, v_sb = w_in[:, :sbw], w_in[:, sbw:2 * sbw], w_in[:, 2 * sbw:3 * sbw]
    o = 3 * sbw
    q_fx, k_fx, v_fx = w_in[:, o:o + fxw], w_in[:, o + fxw:o + 2 * fxw], w_in[:, o + 2 * fxw:o + 3 * fxw]
    wf = w_in[:, o + 3 * fxw:]

    wqT = jnp.concatenate([q_sb, q_fx], axis=1).T.astype(BF16)
    wk = jnp.concatenate([k_sb, k_fx], axis=1).reshape(dm, N_HEADS, HEAD_DIM)
    wk = jnp.pad(wk, ((0, 0), (0, 0), (0, HEAD_PAD - HEAD_DIM)))
    wk = wk.reshape(dm, N_HEADS * HEAD_PAD).astype(BF16)
    wvT = jnp.concatenate([v_sb, v_fx], axis=1).T.astype(BF16)

    wf_hi = wf.astype(BF16)
    wf_lo = (wf - wf_hi.astype(F32)).astype(BF16)
    wf2 = jnp.concatenate([wf_hi, wf_lo], axis=1)

    qT, k, vT, c, kpre, cend = _projection(x2, wqT, wk, wvT, wf2, b_f.reshape(1, N_FX))

    srow = lax.broadcasted_iota(jnp.int32, (BK, BK), 0)
    jcol = lax.broadcasted_iota(jnp.int32, (BK, BK), 1)
    tri = (jcol >= srow).astype(BF16)
    ct = c.T.reshape(N_FX, 1, s)
    kpre = kpre.reshape(s // BK, N_FX).T
    cend = cend.reshape(s // BK, N_FX).T

    oT_sb, oT_fx = _attention(qT, k, vT, tri, ct, kpre, cend,
                              g_sb.reshape(sbw, 1), g_fox.reshape(fxw, 1))

    wo = w_out.astype(BF16)
    return _out_ffn(
        alpha, x2, oT_sb, oT_fx, wo[:sbw], wo[sbw:],
        ln1_g.reshape(1, dm), ln1_b.reshape(1, dm),
        ln2_g.reshape(1, dm), ln2_b.reshape(1, dm),
        w_gate_up.astype(BF16), w_down.astype(BF16))


def kernel(x, w_in, b_f, g_sb, g_fox, w_out, ln1_g, ln1_b, ln2_g, ln2_b, w_gate_up, w_down):
    batch, s, dm = x.shape
    depth = w_in.shape[0]
    alpha = (2 * depth) ** 0.25
    outs = []
    for b in range(batch):
        h = x[b]
        for l in range(depth):
            h = _layer(h, w_in[l], b_f[l], g_sb[l], g_fox[l], w_out[l],
                       ln1_g[l], ln1_b[l], ln2_g[l], ln2_b[l],
                       w_gate_up[l], w_down[l], alpha)
        outs.append(h)
    return outs[0][None] if batch == 1 else jnp.stack(outs, axis=0)
```

```python
import functools

import jax
import jax.numpy as jnp
from jax import lax
from jax.experimental import pallas as pl
from jax.experimental.pallas import tpu as pltpu

F32 = jnp.float32
BF16 = jnp.bfloat16

HEAD_DIM = 64
HEAD_PAD = 128
N_SB = 8
N_FX = 8
N_HEADS = N_SB + N_FX
LN_EPS = 1e-5
RMS_EPS = 1e-6
LOG2E = 1.4426950408889634

BLOCK_ROWS = 512
BQ = 512
BK = 256
NSUB = BQ // BK
FF_CHUNK = 256
VMEM_LIMIT = 56 * 1024 * 1024

SB_DEAD_LOG2 = 160.0
FX_DEAD_LN = 110.0
FX_NORM_SLACK = 1.001

_NT = (((1,), (1,)), ((), ()))
_TN = (((0,), (0,)), ((), ()))


def _const_spec(shape):
    nd = len(shape)
    return pl.BlockSpec(shape, lambda *_: (0,) * nd, pipeline_mode=pl.Buffered(1))


def _proj_kernel(x_ref, wqT_ref, wk_ref, wvT_ref, wf_ref, bf_ref,
                 qT_ref, k_ref, vT_ref, c_ref, kpre_ref, cend_ref,
                 carry_ref, kmax_ref):
    i = pl.program_id(0)
    bs = x_ref.shape[0]
    n_kt = bs // BK

    @pl.when(i == 0)
    def _():
        carry_ref[...] = jnp.zeros_like(carry_ref)
        kmax_ref[...] = jnp.zeros_like(kmax_ref)

    xb = x_ref[...]
    xbf = xb.astype(BF16)

    x_lo = (xb - xbf.astype(F32)).astype(BF16)
    wf2 = wf_ref[...]
    f_hi = jnp.dot(xbf, wf2, preferred_element_type=F32)
    f_lo = jnp.dot(x_lo, wf2[:, :N_FX], preferred_element_type=F32)
    f = f_hi[:, :N_FX] + f_hi[:, N_FX:] + f_lo + bf_ref[...]
    logf = jnp.minimum(f, 0.0) - jnp.log1p(jnp.exp(-jnp.abs(f)))
    row = lax.broadcasted_iota(jnp.int32, logf.shape, 0)
    c = logf
    d = 1
    while d < bs:
        c = c + jnp.where(row >= d, pltpu.roll(c, d, axis=0), 0.0)
        d *= 2
    c = c + carry_ref[...]
    carry_ref[...] = c[bs - 1:bs, :]
    c_ref[...] = c
    stat_tile = lax.broadcasted_iota(jnp.int32, (n_kt, N_FX), 0)
    stat_head = lax.broadcasted_iota(jnp.int32, (n_kt, N_FX), 1)
    c_end = jnp.zeros((n_kt, N_FX), F32)
    for t in range(n_kt):
        last = jnp.broadcast_to(c[(t + 1) * BK - 1:(t + 1) * BK, :], (n_kt, N_FX))
        c_end = jnp.where(stat_tile == t, last, c_end)
    cend_ref[0] = c_end

    lane = lax.broadcasted_iota(jnp.int32, (bs, HEAD_PAD), 1)
    knorm2 = jnp.zeros((n_kt, N_FX), F32)
    for hp in range(N_HEADS // 2):
        r = jnp.dot(xbf, wk_ref[:, hp * 256:(hp + 1) * 256],
                    preferred_element_type=F32)
        for sub in range(2):
            h = 2 * hp + sub
            rh = r[:, sub * HEAD_PAD:(sub + 1) * HEAD_PAD]
            if h >= N_SB:
                hh = h - N_SB
                kr = rh.astype(BF16).astype(F32)
                n2 = jnp.sum(kr * kr, axis=1, keepdims=True)
                for t in range(n_kt):
                    tmax = jnp.max(n2[t * BK:(t + 1) * BK, :], axis=0, keepdims=True)
                    knorm2 = jnp.where((stat_tile == t) & (stat_head == hh),
                                       jnp.broadcast_to(tmax, (n_kt, N_FX)), knorm2)
                cb = jnp.broadcast_to(c[:, hh:hh + 1], (bs, HEAD_PAD))
                hi = cb.astype(BF16).astype(F32)
                r1 = cb - hi
                mid = r1.astype(BF16).astype(F32)
                lo = (r1 - mid).astype(BF16).astype(F32)
                aug = jnp.where(lane == HEAD_DIM, hi,
                                jnp.where(lane == HEAD_DIM + 1, mid,
                                          jnp.where(lane == HEAD_DIM + 2, lo, 0.0)))
                rh = rh + aug
            k_ref[:, h * HEAD_PAD:(h + 1) * HEAD_PAD] = rh.astype(BF16)

    knorm = jnp.sqrt(knorm2)
    k_pre = jnp.zeros((n_kt, N_FX), F32)
    run_max = kmax_ref[...]
    for t in range(n_kt):
        run_max = jnp.maximum(run_max, knorm[t:t + 1, :])
        k_pre = jnp.where(stat_tile == t, jnp.broadcast_to(run_max, (n_kt, N_FX)), k_pre)
    kmax_ref[...] = run_max
    kpre_ref[0] = k_pre

    pad_rows = HEAD_PAD - HEAD_DIM
    prow = lax.broadcasted_iota(jnp.int32, (pad_rows, bs), 0)
    neg_rows = jnp.where(prow < 3, -1.0, 0.0).astype(BF16)
    zero_rows = jnp.zeros((pad_rows, bs), BF16)
    for cidx in range(4):
        r = lax.dot_general(wqT_ref[cidx * 256:(cidx + 1) * 256, :], xbf, _NT,
                            preferred_element_type=F32)
        for sub in range(4):
            h = cidx * 4 + sub
            base = h * HEAD_PAD
            scale = HEAD_DIM ** -0.5 * (LOG2E if h < N_SB else 1.0)
            qT_ref[base:base + HEAD_DIM, :] = (
                r[sub * HEAD_DIM:(sub + 1) * HEAD_DIM, :] * scale).astype(BF16)
            qT_ref[base + HEAD_DIM:base + HEAD_PAD, :] = (
                neg_rows if h >= N_SB else zero_rows)

    for cidx in range(4):
        r = lax.dot_general(wvT_ref[cidx * 256:(cidx + 1) * 256, :], xbf, _NT,
                            preferred_element_type=F32)
        for jb in range(n_kt):
            vT_ref[jb, cidx * 256:(cidx + 1) * 256, :] = (
                r[:, jb * BK:(jb + 1) * BK].astype(BF16))


def _projection(x2, wqT, wk, wvT, wf, bf):
    s, dm = x2.shape
    bs = BLOCK_ROWS
    n_kt = bs // BK
    return pl.pallas_call(
        _proj_kernel,
        grid=(s // bs,),
        in_specs=[
            pl.BlockSpec((bs, dm), lambda i: (i, 0)),
            _const_spec(wqT.shape),
            _const_spec(wk.shape),
            _const_spec(wvT.shape),
            _const_spec(wf.shape),
            _const_spec(bf.shape),
        ],
        out_specs=[
            pl.BlockSpec((N_HEADS * HEAD_PAD, bs), lambda i: (0, i)),
            pl.BlockSpec((bs, N_HEADS * HEAD_PAD), lambda i: (i, 0)),
            pl.BlockSpec((n_kt, N_HEADS * HEAD_DIM, BK), lambda i: (i, 0, 0)),
            pl.BlockSpec((bs, N_FX), lambda i: (i, 0)),
            pl.BlockSpec((1, n_kt, N_FX), lambda i: (i, 0, 0)),
            pl.BlockSpec((1, n_kt, N_FX), lambda i: (i, 0, 0)),
        ],
        out_shape=[
            jax.ShapeDtypeStruct((N_HEADS * HEAD_PAD, s), BF16),
            jax.ShapeDtypeStruct((s, N_HEADS * HEAD_PAD), BF16),
            jax.ShapeDtypeStruct((s // BK, N_HEADS * HEAD_DIM, BK), BF16),
            jax.ShapeDtypeStruct((s, N_FX), F32),
            jax.ShapeDtypeStruct((s // bs, n_kt, N_FX), F32),
            jax.ShapeDtypeStruct((s // bs, n_kt, N_FX), F32),
        ],
        scratch_shapes=[pltpu.VMEM((1, N_FX), F32), pltpu.VMEM((1, N_FX), F32)],
        compiler_params=pltpu.CompilerParams(
            dimension_semantics=("arbitrary",), vmem_limit_bytes=VMEM_LIMIT),
        name="proj",
    )(x2, wqT, wk, wvT, wf, bf)


def _head_rmsnorm_T(oT, g_col):
    ms = jnp.mean(oT * oT, axis=0, keepdims=True)
    return oT * lax.rsqrt(ms + RMS_EPS) * g_col


def _key_before_query(n_queries, strict):
    srow = lax.broadcasted_iota(jnp.int32, (BK, n_queries), 0)
    tcol = lax.broadcasted_iota(jnp.int32, (BK, n_queries), 1)
    return srow < tcol if strict else srow <= tcol


def _key_block(k_ref, j):
    return k_ref[pl.ds(pl.multiple_of(j * BK, BK), BK), :]


def _dot(a, b):
    return jnp.dot(a, b, preferred_element_type=F32)


def _place(x, q_lo, q_hi, fill):
    pieces = []
    if q_lo:
        pieces.append(jnp.full((x.shape[0], q_lo), fill, F32))
    pieces.append(x)
    if q_hi < BQ:
        pieces.append(jnp.full((x.shape[0], BQ - q_hi), fill, F32))
    return pieces[0] if len(pieces) == 1 else jnp.concatenate(pieces, axis=1)


def _widen(x, q0, fill):
    return _place(x, q0, BQ, fill)


class _StickBreaking:
    def __init__(self, i, qT_ref, k_ref, vT_ref, tri_ref, sets, acc_ref, run_ref):
        self.qT_ref, self.k_ref, self.vT_ref, self.tri_ref = qT_ref, k_ref, vT_ref, tri_ref
        self.sets, self.acc_ref, self.run_ref = sets, acc_ref, run_ref
        self.n_tiles = NSUB * (i + 1)

    def key_tile(self, t):
        return self.n_tiles - 1 - t

    def init(self):
        self.acc_ref[...] = jnp.zeros_like(self.acc_ref)
        self.run_ref[...] = jnp.zeros_like(self.run_ref)

    def diag(self):
        return [(0, BK, BQ, _key_before_query(BQ - BK, strict=True)),
                (1, 0, BQ, _key_before_query(BQ, strict=True))]

    @staticmethod
    def softplus2(z):
        return jnp.maximum(z, 0.0) + jnp.log(1.0 + jnp.exp2(-jnp.abs(z))) * LOG2E

    @staticmethod
    def split(sp):
        hi = sp.astype(BF16)
        return hi, (sp - hi.astype(F32)).astype(BF16)

    def suffix_sums(self, hi, lo):
        tri = self.tri_ref[...]
        return _dot(tri, hi) + _dot(tri, lo)

    def scores(self, descs):
        return [_dot(_key_block(self.k_ref, self.key_tile(t)), self.qT_ref[:, q_lo:q_hi])
                for t, q_lo, q_hi, _ in descs]

    def splits(self, descs, zs):
        parts = []
        for desc, z in zip(descs, zs):
            sp = self.softplus2(z)
            if desc[3] is not None:
                sp = jnp.where(desc[3], sp, 0.0)
            parts.append(self.split(sp))
        return parts

    def finish(self, descs, zs, csums):
        ws = []
        for desc, z, csum in zip(descs, zs, csums):
            w = jnp.exp2(z - csum)
            if desc[3] is not None:
                w = jnp.where(desc[3], w, 0.0)
            ws.append(w.astype(BF16))
        pvs = [_dot(self.vT_ref[self.key_tile(desc[0])], w) for desc, w in zip(descs, ws)]
        run = self.run_ref[...]
        acc = self.acc_ref[...]
        for (_, q_lo, q_hi, _), csum, pv in zip(descs, csums, pvs):
            acc = acc + _place(pv, q_lo, q_hi, 0.0) * jnp.exp2(-run)
            run = run + _place(csum[0:1, :], q_lo, q_hi, 0.0)
        self.acc_ref[...] = acc
        self.run_ref[...] = run

    def batch(self, descs):
        zs = self.scores(descs)
        parts = self.splits(descs, zs)
        self.finish(descs, zs, [self.suffix_sums(hi, lo) for hi, lo in parts])

    def step(self, score_t=None, csum_t=None, pv_t=None):
        if score_t is not None:
            z = _dot(_key_block(self.k_ref, self.key_tile(score_t[0])), self.qT_ref[...])
        if csum_t is not None:
            z_ref, hi_ref, lo_ref, w_ref, r_ref = self.sets[csum_t[1]]
            csum = self.suffix_sums(hi_ref[...], lo_ref[...])
        if pv_t is not None:
            pv_set = self.sets[pv_t[1]]
            self.acc_ref[...] += pv_set[4][...] * _dot(
                self.vT_ref[self.key_tile(pv_t[0])], pv_set[3][...])
        if score_t is not None:
            zs_ref, his_ref, los_ref, _, _ = self.sets[score_t[1]]
            hi, lo = self.split(self.softplus2(z))
            zs_ref[...] = z
            his_ref[...] = hi
            los_ref[...] = lo
        if csum_t is not None:
            w_ref[...] = jnp.exp2(z_ref[...] - csum).astype(BF16)
            run = self.run_ref[...]
            r_ref[...] = jnp.exp2(-run)
            self.run_ref[...] = run + csum[0:1, :]

    def alive_in(self, q_lo, q_hi):
        return jnp.min(self.run_ref[:, q_lo:q_hi]) < SB_DEAD_LOG2

    def alive(self):
        return self.alive_in(0, BQ)

    def rest(self, enter):
        @pl.when(enter)
        def _():
            self._rest_if_alive()

    def _rest_if_alive(self):
        @pl.when(self.alive())
        def _():
            self.batch([(3, 0, BQ, None)])

            @pl.when(jnp.logical_and(self.alive(), self.n_tiles > 4))
            def _():
                self.step(score_t=(4, 0))
                self.step(score_t=(5, 1), csum_t=(4, 0))

                def cond(carry):
                    return jnp.logical_and(carry[0] < self.n_tiles, carry[1] > 0)

                def body(carry):
                    t = carry[0]
                    still_alive = self.alive().astype(jnp.int32)
                    self.step(score_t=(t, 0), csum_t=(t - 1, 1), pv_t=(t - 2, 0))
                    self.step(score_t=(t + 1, 1), csum_t=(t, 0), pv_t=(t - 1, 1))
                    return t + 2, still_alive

                t_end, _ = lax.while_loop(cond, body, (jnp.int32(6), jnp.int32(1)))
                self.step(csum_t=(t_end - 1, 1), pv_t=(t_end - 2, 0))
                self.step(pv_t=(t_end - 1, 1))


class _Forgetting:
    def __init__(self, h, i, kpre_ref, cend_ref, qT_ref, k_ref, vT_ref, ct_ref, sets,
                 acc_ref, m_ref, l_ref, qn_ref):
        self.h, self.kpre_ref, self.cend_ref = h, kpre_ref, cend_ref
        self.qT_ref, self.k_ref, self.vT_ref, self.ct_ref = qT_ref, k_ref, vT_ref, ct_ref
        self.sets, self.acc_ref, self.m_ref, self.l_ref, self.qn_ref = (
            sets, acc_ref, m_ref, l_ref, qn_ref)
        self.n_tiles = NSUB * (i + 1)
        self.d0 = NSUB * i

    def key_tile(self, t):
        return self.n_tiles - 1 - t

    def init(self):
        self.acc_ref[...] = jnp.zeros_like(self.acc_ref)
        self.l_ref[...] = jnp.zeros_like(self.l_ref)
        self.m_ref[...] = jnp.full(self.m_ref.shape, -jnp.inf, F32)

    def set_query_norms(self):
        qf = self.qT_ref[0:HEAD_DIM, :].astype(F32)
        self.qn_ref[...] = jnp.sqrt(jnp.sum(qf * qf, axis=0, keepdims=True))

    def diag(self):
        return [(self.d0, 0, _key_before_query(BQ, strict=False)),
                (self.d0 + 1, BK, _key_before_query(BQ - BK, strict=False))]

    def scores(self, descs):
        return [_dot(_key_block(self.k_ref, j), self.qT_ref[:, q0:]) for j, q0, _ in descs]

    def soft_group(self, descs, us):
        us = [u if mask is None else jnp.where(mask, u, -jnp.inf)
              for (_, _, mask), u in zip(descs, us)]
        mx = None
        for (_, q0, _), u in zip(descs, us):
            cm = _widen(jnp.max(u, axis=0, keepdims=True), q0, -jnp.inf)
            mx = cm if mx is None else jnp.maximum(mx, cm)
        ct = self.ct_ref[...]
        m_old = self.m_ref[...]
        m_new = jnp.maximum(m_old, mx + ct)
        off = ct - m_new
        alpha = jnp.exp(m_old - m_new)
        l = alpha * self.l_ref[...]
        acc = alpha * self.acc_ref[...]
        for (j, q0, _), u in zip(descs, us):
            p = jnp.exp(u + off[:, q0:])
            l = l + _widen(jnp.sum(p, axis=0, keepdims=True), q0, 0.0)
            acc = acc + _widen(_dot(self.vT_ref[j], p.astype(BF16)), q0, 0.0)
        self.m_ref[...] = m_new
        self.l_ref[...] = l
        self.acc_ref[...] = acc

    def batch(self, groups):
        scores = [self.scores(descs) for descs in groups]
        for descs, us in zip(groups, scores):
            self.soft_group(descs, us)

    def step(self, score_t=None, soft_t=None, pv_t=None):
        if score_t is not None:
            u = _dot(_key_block(self.k_ref, score_t[0]), self.qT_ref[...])
        if pv_t is not None:
            pv_set = self.sets[pv_t[1]]
            self.acc_ref[...] = pv_set[3][...] * self.acc_ref[...] + _dot(
                self.vT_ref[pv_t[0]], pv_set[2][...])
        if score_t is not None:
            us_ref, mxs_ref, _, _ = self.sets[score_t[1]]
            us_ref[...] = u
            mxs_ref[...] = jnp.max(u, axis=0, keepdims=True)
        if soft_t is not None:
            u_ref, mx_ref, p_ref, al_ref = self.sets[soft_t[1]]
            ct = self.ct_ref[...]
            m_old = self.m_ref[...]
            m_new = jnp.maximum(m_old, mx_ref[...] + ct)
            p = jnp.exp(u_ref[...] + (ct - m_new))
            alpha = jnp.exp(m_old - m_new)
            self.m_ref[...] = m_new
            self.l_ref[...] = alpha * self.l_ref[...] + jnp.sum(p, axis=0, keepdims=True)
            p_ref[...] = p.astype(BF16)
            al_ref[...] = alpha

    def alive(self, t):
        j = jnp.maximum(self.key_tile(t), 0)
        kmax = self.kpre_ref[self.h, j]
        c_end = self.cend_ref[self.h, j]
        bound = (self.qn_ref[...] * (kmax * FX_NORM_SLACK)
                 + (self.ct_ref[...] - c_end) - self.m_ref[...])
        return jnp.max(bound) >= -(FX_DEAD_LN + 1e-5 * jnp.abs(c_end))

    def more_after_four(self):
        return jnp.logical_and(self.alive(4), self.n_tiles > 4)

    def rest(self, enter):
        key_tile = self.key_tile

        @pl.when(enter)
        def _():
            self.step(score_t=(key_tile(4), 0))
            self.step(score_t=(key_tile(5), 1), soft_t=(key_tile(4), 0))

            def cond(carry):
                return jnp.logical_and(carry[0] < self.n_tiles, carry[1] > 0)

            def body(carry):
                t = carry[0]
                next_alive = self.alive(t + 2).astype(jnp.int32)
                self.step(score_t=(key_tile(t), 0), soft_t=(key_tile(t - 1), 1),
                          pv_t=(key_tile(t - 2), 0))
                self.step(score_t=(key_tile(t + 1), 1), soft_t=(key_tile(t), 0),
                          pv_t=(key_tile(t - 1), 1))
                return t + 2, next_alive

            t_end, _ = lax.while_loop(
                cond, body, (jnp.int32(6), self.alive(6).astype(jnp.int32)))
            self.step(soft_t=(key_tile(t_end - 1), 1), pv_t=(key_tile(t_end - 2), 0))
            self.step(pv_t=(key_tile(t_end - 1), 1))


Q_BLOCKS_PER_STEP = 2


def _fast_tiles(sb, fx):
    fx.set_query_norms()
    s_descs = sb.diag() + [(2, 0, BK, None)]
    f_groups = [fx.diag(), [(fx.key_tile(2), 0, None), (fx.key_tile(3), 0, None)]]
    s_z = sb.scores(s_descs)
    f_u = [fx.scores(descs) for descs in f_groups]
    parts = sb.splits(s_descs, s_z)
    csums = [sb.suffix_sums(*parts[0])]
    fx.soft_group(f_groups[0], f_u[0])
    csums += [sb.suffix_sums(hi, lo) for hi, lo in parts[1:]]
    fx.soft_group(f_groups[1], f_u[1])
    sb.finish(s_descs, s_z, csums)


def _remaining_tiles(sb, fx):
    late_alive = sb.alive_in(BK, BQ)
    early_alive = sb.alive_in(0, BK)
    fx_more = fx.more_after_four()

    @pl.when(late_alive)
    def _():
        sb.batch([(2, BK, BQ, None)])

    sb.rest(jnp.logical_or(late_alive, early_alive))
    fx.rest(fx_more)


def _attn_kernel(kpre_ref, cend_ref, qs_ref, qf_ref, ks_ref, kf_ref, vs_ref, vf_ref,
                 tri_ref, ct_ref, gs_ref, gf_ref, os_ref, of_ref, *scratch):
    h = pl.program_id(0)
    step = pl.program_id(1)
    sb_sets = (scratch[0:5], scratch[5:10])
    fx_sets = (scratch[10:14], scratch[14:18])
    state = scratch[18:]
    blocks = []
    for slot in range(Q_BLOCKS_PER_STEP):
        i = step * Q_BLOCKS_PER_STEP + slot
        lanes = pl.ds(slot * BQ, BQ)
        sb_acc, sb_run, fx_acc, fx_m, fx_l, fx_qn = state[6 * slot:6 * slot + 6]
        sb = _StickBreaking(i, qs_ref.at[:, lanes], ks_ref, vs_ref, tri_ref, sb_sets,
                            sb_acc, sb_run)
        fx = _Forgetting(h, i, kpre_ref, cend_ref, qf_ref.at[:, lanes], kf_ref, vf_ref,
                         ct_ref.at[:, lanes], fx_sets, fx_acc, fx_m, fx_l, fx_qn)
        sb.init()
        fx.init()
        blocks.append((sb, fx))

    @pl.when(step == 0)
    def _():
        sb0, fx0 = blocks[0]
        sb0.batch(sb0.diag())
        fx0.batch([fx0.diag()])
        for sb, fx in blocks[1:]:
            _fast_tiles(sb, fx)
            _remaining_tiles(sb, fx)

    @pl.when(step > 0)
    def _():
        for sb, fx in blocks:
            _fast_tiles(sb, fx)
            _remaining_tiles(sb, fx)

    for slot, (sb, fx) in enumerate(blocks):
        lanes = slice(slot * BQ, (slot + 1) * BQ)
        os_ref[:, lanes] = _head_rmsnorm_T(sb.acc_ref[...], gs_ref[...]).astype(os_ref.dtype)
        of_ref[:, lanes] = _head_rmsnorm_T(fx.acc_ref[...] / fx.l_ref[...],
                                           gf_ref[...]).astype(of_ref.dtype)


def _attn_scratch():
    tile_f32 = pltpu.VMEM((BK, BQ), F32)
    tile_bf16 = pltpu.VMEM((BK, BQ), BF16)
    row = pltpu.VMEM((1, BQ), F32)
    acc = pltpu.VMEM((HEAD_DIM, BQ), F32)
    sb_set = [tile_f32, tile_bf16, tile_bf16, tile_bf16, row]
    fx_set = [tile_f32, row, tile_bf16, row]
    per_block = [acc, row, acc, row, row, row]
    return sb_set + sb_set + fx_set + fx_set + per_block * Q_BLOCKS_PER_STEP


def _attention(qT, k, vT, tri, ct, kpre, cend, g_sb_col, g_fx_col):
    s = k.shape[0]
    bq = BQ * Q_BLOCKS_PER_STEP
    nkb = s // BK
    smem_spec = pl.BlockSpec(memory_space=pltpu.SMEM)
    out_spec = pl.BlockSpec((HEAD_DIM, bq), lambda h, i: (h, i))
    out_shape = jax.ShapeDtypeStruct((N_SB * HEAD_DIM, s), BF16)
    return pl.pallas_call(
        _attn_kernel,
        grid=(N_SB, s // bq),
        in_specs=[
            smem_spec, smem_spec,
            pl.BlockSpec((HEAD_PAD, bq), lambda h, i: (h, i)),
            pl.BlockSpec((HEAD_PAD, bq), lambda h, i: (N_SB + h, i)),
            pl.BlockSpec((s, HEAD_PAD), lambda h, i: (0, h)),
            pl.BlockSpec((s, HEAD_PAD), lambda h, i: (0, N_SB + h)),
            pl.BlockSpec((nkb, HEAD_DIM, BK), lambda h, i: (0, h, 0)),
            pl.BlockSpec((nkb, HEAD_DIM, BK), lambda h, i: (0, N_SB + h, 0)),
            pl.BlockSpec((BK, BK), lambda h, i: (0, 0)),
            pl.BlockSpec((None, 1, bq), lambda h, i: (h, 0, i)),
            pl.BlockSpec((HEAD_DIM, 1), lambda h, i: (h, 0)),
            pl.BlockSpec((HEAD_DIM, 1), lambda h, i: (h, 0)),
        ],
        out_specs=[out_spec, out_spec],
        out_shape=[out_shape, out_shape],
        scratch_shapes=_attn_scratch(),
        compiler_params=pltpu.CompilerParams(
            dimension_semantics=("arbitrary", "arbitrary"),
            vmem_limit_bytes=VMEM_LIMIT),
        name="attn",
    )(kpre, cend, qT, qT, k, k, vT, vT, tri, ct, g_sb_col, g_fx_col)


def _layer_norm(v, g, b):
    mu = jnp.mean(v, axis=-1, keepdims=True)
    vc = v - mu
    var = jnp.mean(vc * vc, axis=-1, keepdims=True)
    return vc * lax.rsqrt(var + LN_EPS) * g + b


def _ffn_kernel(alpha, x_ref, osb_ref, ofx_ref, woa_ref, wob_ref,
                g1_ref, b1_ref, g2_ref, b2_ref, wgu_ref, wd_ref,
                y_ref, act_ref):
    d_ff = wd_ref.shape[0]
    mix = (lax.dot_general(osb_ref[...], woa_ref[...], _TN, preferred_element_type=F32)
           + lax.dot_general(ofx_ref[...], wob_ref[...], _TN, preferred_element_type=F32))
    h1 = _layer_norm(alpha * x_ref[...] + mix, g1_ref[...], b1_ref[...])
    h1b = h1.astype(BF16)
    for c in range(d_ff // FF_CHUNK):
        lo, hi = c * FF_CHUNK, (c + 1) * FF_CHUNK
        gate = jnp.dot(h1b, wgu_ref[:, lo:hi], preferred_element_type=F32)
        up = jnp.dot(h1b, wgu_ref[:, d_ff + lo:d_ff + hi], preferred_element_type=F32)
        act_ref[:, lo:hi] = (gate * jax.nn.sigmoid(gate) * up).astype(BF16)
    ff = jnp.dot(act_ref[...], wd_ref[...], preferred_element_type=F32)
    y_ref[...] = _layer_norm(alpha * h1 + ff, g2_ref[...], b2_ref[...])


def _out_ffn(alpha, x2, oT_sb, oT_fx, wo_a, wo_b, g1, b1, g2, b2, wgu, wd):
    s, dm = x2.shape
    bs = BLOCK_ROWS
    d_ff = wd.shape[0]
    half = oT_sb.shape[0]
    return pl.pallas_call(
        functools.partial(_ffn_kernel, alpha),
        grid=(s // bs,),
        in_specs=[
            pl.BlockSpec((bs, dm), lambda i: (i, 0)),
            pl.BlockSpec((half, bs), lambda i: (0, i)),
            pl.BlockSpec((half, bs), lambda i: (0, i)),
            _const_spec(wo_a.shape),
            _const_spec(wo_b.shape),
            _const_spec(g1.shape),
            _const_spec(b1.shape),
            _const_spec(g2.shape),
            _const_spec(b2.shape),
            _const_spec(wgu.shape),
            _const_spec(wd.shape),
        ],
        out_specs=pl.BlockSpec((bs, dm), lambda i: (i, 0)),
        out_shape=jax.ShapeDtypeStruct((s, dm), F32),
        scratch_shapes=[pltpu.VMEM((bs, d_ff), BF16)],
        compiler_params=pltpu.CompilerParams(
            dimension_semantics=("arbitrary",), vmem_limit_bytes=VMEM_LIMIT),
        name="out_ffn",
    )(x2, oT_sb, oT_fx, wo_a, wo_b, g1, b1, g2, b2, wgu, wd)


def _layer(x2, w_in, b_f, g_sb, g_fox, w_out, ln1_g, ln1_b, ln2_g, ln2_b,
           w_gate_up, w_down, alpha):
    assert NSUB == 2
    s, dm = x2.shape
    sbw = N_SB * HEAD_DIM
    fxw = N_FX * HEAD_DIM
    q_sb, k_sb, v_sb = w_in[:, :sbw], w_in[:, sbw:2 * sbw], w_in[:, 2 * sbw:3 * sbw]
    o = 3 * sbw
    q_fx, k_fx, v_fx = w_in[:, o:o + fxw], w_in[:, o + fxw:o + 2 * fxw], w_in[:, o + 2 * fxw:o + 3 * fxw]
    wf = w_in[:, o + 3 * fxw:]

    wqT = jnp.concatenate([q_sb, q_fx], axis=1).T.astype(BF16)
    wk = jnp.concatenate([k_sb, k_fx], axis=1).reshape(dm, N_HEADS, HEAD_DIM)
    wk = jnp.pad(wk, ((0, 0), (0, 0), (0, HEAD_PAD - HEAD_DIM)))
    wk = wk.reshape(dm, N_HEADS * HEAD_PAD).astype(BF16)
    wvT = jnp.concatenate([v_sb, v_fx], axis=1).T.astype(BF16)

    wf_hi = wf.astype(BF16)
    wf_lo = (wf - wf_hi.astype(F32)).astype(BF16)
    wf2 = jnp.concatenate([wf_hi, wf_lo], axis=1)

    qT, k, vT, c, kpre, cend = _projection(x2, wqT, wk, wvT, wf2, b_f.reshape(1, N_FX))

    srow = lax.broadcasted_iota(jnp.int32, (BK, BK), 0)
    jcol = lax.broadcasted_iota(jnp.int32, (BK, BK), 1)
    tri = (jcol >= srow).astype(BF16)
    ct = c.T.reshape(N_FX, 1, s)
    kpre = kpre.reshape(s // BK, N_FX).T
    cend = cend.reshape(s // BK, N_FX).T

    oT_sb, oT_fx = _attention(qT, k, vT, tri, ct, kpre, cend,
                              g_sb.reshape(sbw, 1), g_fox.reshape(fxw, 1))

    wo = w_out.astype(BF16)
    return _out_ffn(
        alpha, x2, oT_sb, oT_fx, wo[:sbw], wo[sbw:],
        ln1_g.reshape(1, dm), ln1_b.reshape(1, dm),
        ln2_g.reshape(1, dm), ln2_b.reshape(1, dm),
        w_gate_up.astype(BF16), w_down.astype(BF16))


def kernel(x, w_in, b_f, g_sb, g_fox, w_out, ln1_g, ln1_b, ln2_g, ln2_b, w_gate_up, w_down):
    batch, s, dm = x.shape
    depth = w_in.shape[0]
    alpha = (2 * depth) ** 0.25
    outs = []
    for b in range(batch):
        h = x[b]
        for l in range(depth):
            h = _layer(h, w_in[l], b_f[l], g_sb[l], g_fox[l], w_out[l],
                       ln1_g[l], ln1_b[l], ln2_g[l], ln2_b[l],
                       w_gate_up[l], w_down[l], alpha)
        outs.append(h)
    return outs[0][None] if batch == 1 else jnp.stack(outs, axis=0)
```

```python
import functools

import jax
import jax.numpy as jnp
from jax import lax
from jax.experimental import pallas as pl
from jax.experimental.pallas import tpu as pltpu

F32 = jnp.float32
BF16 = jnp.bfloat16

HEAD_DIM = 64
HEAD_PAD = 128
N_SB = 8
N_FX = 8
N_HEADS = N_SB + N_FX
SB_K_WIDTH = N_SB * HEAD_DIM
K_WIDTH = SB_K_WIDTH + N_FX * HEAD_PAD
LN_EPS = 1e-5
RMS_EPS = 1e-6
LOG2E = 1.4426950408889634

BLOCK_ROWS = 512
BQ = 512
BK = 256
NSUB = BQ // BK
FF_CHUNK = 256
VMEM_LIMIT = 56 * 1024 * 1024

SB_DEAD_LOG2 = 160.0
FX_DEAD_LN = 110.0
FX_NORM_SLACK = 1.001

_NT = (((1,), (1,)), ((), ()))
_TN = (((0,), (0,)), ((), ()))


def _const_spec(shape):
    nd = len(shape)
    return pl.BlockSpec(shape, lambda *_: (0,) * nd, pipeline_mode=pl.Buffered(1))


def _proj_kernel(x_ref, wqT_ref, wks_ref, wkf_ref, wvT_ref, wf_ref, bf_ref,
                 qT_ref, k_ref, vT_ref, c_ref, kpre_ref, cend_ref,
                 carry_ref, kmax_ref):
    i = pl.program_id(0)
    bs = x_ref.shape[0]
    n_kt = bs // BK

    @pl.when(i == 0)
    def _():
        carry_ref[...] = jnp.zeros_like(carry_ref)
        kmax_ref[...] = jnp.zeros_like(kmax_ref)

    xb = x_ref[...]
    xbf = xb.astype(BF16)

    x_lo = (xb - xbf.astype(F32)).astype(BF16)
    wf2 = wf_ref[...]
    f_hi = jnp.dot(xbf, wf2, preferred_element_type=F32)
    f_lo = jnp.dot(x_lo, wf2[:, :N_FX], preferred_element_type=F32)
    f = f_hi[:, :N_FX] + f_hi[:, N_FX:] + f_lo + bf_ref[...]
    logf = jnp.minimum(f, 0.0) - jnp.log1p(jnp.exp(-jnp.abs(f)))
    row = lax.broadcasted_iota(jnp.int32, logf.shape, 0)
    c = logf
    d = 1
    while d < bs:
        c = c + jnp.where(row >= d, pltpu.roll(c, d, axis=0), 0.0)
        d *= 2
    c = c + carry_ref[...]
    carry_ref[...] = c[bs - 1:bs, :]
    c_ref[...] = c
    stat_tile = lax.broadcasted_iota(jnp.int32, (n_kt, N_FX), 0)
    stat_head = lax.broadcasted_iota(jnp.int32, (n_kt, N_FX), 1)
    c_end = jnp.zeros((n_kt, N_FX), F32)
    for t in range(n_kt):
        last = jnp.broadcast_to(c[(t + 1) * BK - 1:(t + 1) * BK, :], (n_kt, N_FX))
        c_end = jnp.where(stat_tile == t, last, c_end)
    cend_ref[0] = c_end

    for cidx in range(SB_K_WIDTH // 256):
        cols = slice(cidx * 256, (cidx + 1) * 256)
        k_ref[:, cols] = jnp.dot(xbf, wks_ref[:, cols],
                                 preferred_element_type=F32).astype(BF16)

    lane = lax.broadcasted_iota(jnp.int32, (bs, HEAD_PAD), 1)
    knorm2 = jnp.zeros((n_kt, N_FX), F32)
    for hp in range(N_FX // 2):
        r = jnp.dot(xbf, wkf_ref[:, hp * 256:(hp + 1) * 256],
                    preferred_element_type=F32)
        for sub in range(2):
            hh = 2 * hp + sub
            rh = r[:, sub * HEAD_PAD:(sub + 1) * HEAD_PAD]
            kr = rh.astype(BF16).astype(F32)
            n2 = jnp.sum(kr * kr, axis=1, keepdims=True)
            for t in range(n_kt):
                tmax = jnp.max(n2[t * BK:(t + 1) * BK, :], axis=0, keepdims=True)
                knorm2 = jnp.where((stat_tile == t) & (stat_head == hh),
                                   jnp.broadcast_to(tmax, (n_kt, N_FX)), knorm2)
            cb = jnp.broadcast_to(c[:, hh:hh + 1], (bs, HEAD_PAD))
            hi = cb.astype(BF16).astype(F32)
            r1 = cb - hi
            mid = r1.astype(BF16).astype(F32)
            lo = (r1 - mid).astype(BF16).astype(F32)
            aug = jnp.where(lane == HEAD_DIM, hi,
                            jnp.where(lane == HEAD_DIM + 1, mid,
                                      jnp.where(lane == HEAD_DIM + 2, lo, 0.0)))
            base = SB_K_WIDTH + hh * HEAD_PAD
            k_ref[:, base:base + HEAD_PAD] = (rh + aug).astype(BF16)

    knorm = jnp.sqrt(knorm2)
    k_pre = jnp.zeros((n_kt, N_FX), F32)
    run_max = kmax_ref[...]
    for t in range(n_kt):
        run_max = jnp.maximum(run_max, knorm[t:t + 1, :])
        k_pre = jnp.where(stat_tile == t, jnp.broadcast_to(run_max, (n_kt, N_FX)), k_pre)
    kmax_ref[...] = run_max
    kpre_ref[0] = k_pre

    pad_rows = HEAD_PAD - HEAD_DIM
    prow = lax.broadcasted_iota(jnp.int32, (pad_rows, bs), 0)
    neg_rows = jnp.where(prow < 3, -1.0, 0.0).astype(BF16)
    zero_rows = jnp.zeros((pad_rows, bs), BF16)
    for cidx in range(4):
        r = lax.dot_general(wqT_ref[cidx * 256:(cidx + 1) * 256, :], xbf, _NT,
                            preferred_element_type=F32)
        for sub in range(4):
            h = cidx * 4 + sub
            base = h * HEAD_PAD
            q_rows = r[sub * HEAD_DIM:(sub + 1) * HEAD_DIM, :]
            if h < N_SB:
                own = base + (h % 2) * HEAD_DIM
                other = base + (1 - h % 2) * HEAD_DIM
                qT_ref[own:own + HEAD_DIM, :] = (
                    q_rows * (HEAD_DIM ** -0.5 * LOG2E)).astype(BF16)
                qT_ref[other:other + HEAD_DIM, :] = zero_rows
            else:
                qT_ref[base:base + HEAD_DIM, :] = (q_rows * HEAD_DIM ** -0.5).astype(BF16)
                qT_ref[base + HEAD_DIM:base + HEAD_PAD, :] = neg_rows

    for cidx in range(4):
        r = lax.dot_general(wvT_ref[cidx * 256:(cidx + 1) * 256, :], xbf, _NT,
                            preferred_element_type=F32)
        for jb in range(n_kt):
            vT_ref[jb, cidx * 256:(cidx + 1) * 256, :] = (
                r[:, jb * BK:(jb + 1) * BK].astype(BF16))


def _projection(x2, wqT, wks, wkf, wvT, wf, bf):
    s, dm = x2.shape
    bs = BLOCK_ROWS
    n_kt = bs // BK
    return pl.pallas_call(
        _proj_kernel,
        grid=(s // bs,),
        in_specs=[
            pl.BlockSpec((bs, dm), lambda i: (i, 0)),
            _const_spec(wqT.shape),
            _const_spec(wks.shape),
            _const_spec(wkf.shape),
            _const_spec(wvT.shape),
            _const_spec(wf.shape),
            _const_spec(bf.shape),
        ],
        out_specs=[
            pl.BlockSpec((N_HEADS * HEAD_PAD, bs), lambda i: (0, i)),
            pl.BlockSpec((bs, K_WIDTH), lambda i: (i, 0)),
            pl.BlockSpec((n_kt, N_HEADS * HEAD_DIM, BK), lambda i: (i, 0, 0)),
            pl.BlockSpec((bs, N_FX), lambda i: (i, 0)),
            pl.BlockSpec((1, n_kt, N_FX), lambda i: (i, 0, 0)),
            pl.BlockSpec((1, n_kt, N_FX), lambda i: (i, 0, 0)),
        ],
        out_shape=[
            jax.ShapeDtypeStruct((N_HEADS * HEAD_PAD, s), BF16),
            jax.ShapeDtypeStruct((s, K_WIDTH), BF16),
            jax.ShapeDtypeStruct((s // BK, N_HEADS * HEAD_DIM, BK), BF16),
            jax.ShapeDtypeStruct((s, N_FX), F32),
            jax.ShapeDtypeStruct((s // bs, n_kt, N_FX), F32),
            jax.ShapeDtypeStruct((s // bs, n_kt, N_FX), F32),
        ],
        scratch_shapes=[pltpu.VMEM((1, N_FX), F32), pltpu.VMEM((1, N_FX), F32)],
        compiler_params=pltpu.CompilerParams(
            dimension_semantics=("arbitrary",), vmem_limit_bytes=VMEM_LIMIT),
        name="proj",
    )(x2, wqT, wks, wkf, wvT, wf, bf)


def _head_rmsnorm_T(oT, g_col):
    ms = jnp.mean(oT * oT, axis=0, keepdims=True)
    return oT * lax.rsqrt(ms + RMS_EPS) * g_col


def _key_before_query(n_queries, strict):
    srow = lax.broadcasted_iota(jnp.int32, (BK, n_queries), 0)
    tcol = lax.broadcasted_iota(jnp.int32, (BK, n_queries), 1)
    return srow < tcol if strict else srow <= tcol


def _key_block(k_ref, j):
    return k_ref[pl.ds(pl.multiple_of(j * BK, BK), BK), :]


def _dot(a, b):
    return jnp.dot(a, b, preferred_element_type=F32)


def _place(x, q_lo, q_hi, fill):
    pieces = []
    if q_lo:
        pieces.append(jnp.full((x.shape[0], q_lo), fill, F32))
    pieces.append(x)
    if q_hi < BQ:
        pieces.append(jnp.full((x.shape[0], BQ - q_hi), fill, F32))
    return pieces[0] if len(pieces) == 1 else jnp.concatenate(pieces, axis=1)


def _widen(x, q0, fill):
    return _place(x, q0, BQ, fill)


class _StickBreaking:
    def __init__(self, i, qT_ref, k_ref, vT_ref, tri_ref, sets, acc_ref, run_ref):
        self.qT_ref, self.k_ref, self.vT_ref, self.tri_ref = qT_ref, k_ref, vT_ref, tri_ref
        self.sets, self.acc_ref, self.run_ref = sets, acc_ref, run_ref
        self.n_tiles = NSUB * (i + 1)

    def key_tile(self, t):
        return self.n_tiles - 1 - t

    def init(self):
        self.acc_ref[...] = jnp.zeros_like(self.acc_ref)
        self.run_ref[...] = jnp.zeros_like(self.run_ref)

    def diag(self):
        return [(0, BK, BQ, _key_before_query(BQ - BK, strict=True)),
                (1, 0, BQ, _key_before_query(BQ, strict=True))]

    @staticmethod
    def softplus2(z):
        return jnp.maximum(z, 0.0) + jnp.log(1.0 + jnp.exp2(-jnp.abs(z))) * LOG2E

    @staticmethod
    def split(sp):
        hi = sp.astype(BF16)
        return hi, (sp - hi.astype(F32)).astype(BF16)

    def suffix_sums(self, hi, lo):
        tri = self.tri_ref[...]
        return _dot(tri, hi) + _dot(tri, lo)

    def scores(self, descs):
        return [_dot(_key_block(self.k_ref, self.key_tile(t)), self.qT_ref[:, q_lo:q_hi])
                for t, q_lo, q_hi, _ in descs]

    def splits(self, descs, zs):
        parts = []
        for desc, z in zip(descs, zs):
            sp = self.softplus2(z)
            if desc[3] is not None:
                sp = jnp.where(desc[3], sp, 0.0)
            parts.append(self.split(sp))
        return parts

    def finish(self, descs, zs, csums):
        ws = []
        for desc, z, csum in zip(descs, zs, csums):
            w = jnp.exp2(z - csum)
            if desc[3] is not None:
                w = jnp.where(desc[3], w, 0.0)
            ws.append(w.astype(BF16))
        pvs = [_dot(self.vT_ref[self.key_tile(desc[0])], w) for desc, w in zip(descs, ws)]
        run = self.run_ref[...]
        acc = self.acc_ref[...]
        for (_, q_lo, q_hi, _), csum, pv in zip(descs, csums, pvs):
            acc = acc + _place(pv, q_lo, q_hi, 0.0) * jnp.exp2(-run)
            run = run + _place(csum[0:1, :], q_lo, q_hi, 0.0)
        self.acc_ref[...] = acc
        self.run_ref[...] = run

    def batch(self, descs):
        zs = self.scores(descs)
        parts = self.splits(descs, zs)
        self.finish(descs, zs, [self.suffix_sums(hi, lo) for hi, lo in parts])

    def step(self, score_t=None, csum_t=None, pv_t=None):
        if score_t is not None:
            z = _dot(_key_block(self.k_ref, self.key_tile(score_t[0])), self.qT_ref[...])
        if csum_t is not None:
            z_ref, hi_ref, lo_ref, w_ref, r_ref = self.sets[csum_t[1]]
            csum = self.suffix_sums(hi_ref[...], lo_ref[...])
        if pv_t is not None:
            pv_set = self.sets[pv_t[1]]
            self.acc_ref[...] += pv_set[4][...] * _dot(
                self.vT_ref[self.key_tile(pv_t[0])], pv_set[3][...])
        if score_t is not None:
            zs_ref, his_ref, los_ref, _, _ = self.sets[score_t[1]]
            hi, lo = self.split(self.softplus2(z))
            zs_ref[...] = z
            his_ref[...] = hi
            los_ref[...] = lo
        if csum_t is not None:
            w_ref[...] = jnp.exp2(z_ref[...] - csum).astype(BF16)
            run = self.run_ref[...]
            r_ref[...] = jnp.exp2(-run)
            self.run_ref[...] = run + csum[0:1, :]

    def alive_in(self, q_lo, q_hi):
        return jnp.min(self.run_ref[:, q_lo:q_hi]) < SB_DEAD_LOG2

    def alive(self):
        return self.alive_in(0, BQ)

    def rest(self, enter):
        @pl.when(enter)
        def _():
            self._rest_if_alive()

    def _rest_if_alive(self):
        @pl.when(self.alive())
        def _():
            self.batch([(3, 0, BQ, None)])

            @pl.when(jnp.logical_and(self.alive(), self.n_tiles > 4))
            def _():
                self.step(score_t=(4, 0))
                self.step(score_t=(5, 1), csum_t=(4, 0))

                def cond(carry):
                    return jnp.logical_and(carry[0] < self.n_tiles, carry[1] > 0)

                def body(carry):
                    t = carry[0]
                    still_alive = self.alive().astype(jnp.int32)
                    self.step(score_t=(t, 0), csum_t=(t - 1, 1), pv_t=(t - 2, 0))
                    self.step(score_t=(t + 1, 1), csum_t=(t, 0), pv_t=(t - 1, 1))
                    return t + 2, still_alive

                t_end, _ = lax.while_loop(cond, body, (jnp.int32(6), jnp.int32(1)))
                self.step(csum_t=(t_end - 1, 1), pv_t=(t_end - 2, 0))
                self.step(pv_t=(t_end - 1, 1))


class _Forgetting:
    def __init__(self, h, i, kpre_ref, cend_ref, qT_ref, k_ref, vT_ref, ct_ref, sets,
                 acc_ref, m_ref, l_ref, qn_ref):
        self.h, self.kpre_ref, self.cend_ref = h, kpre_ref, cend_ref
        self.qT_ref, self.k_ref, self.vT_ref, self.ct_ref = qT_ref, k_ref, vT_ref, ct_ref
        self.sets, self.acc_ref, self.m_ref, self.l_ref, self.qn_ref = (
            sets, acc_ref, m_ref, l_ref, qn_ref)
        self.n_tiles = NSUB * (i + 1)
        self.d0 = NSUB * i

    def key_tile(self, t):
        return self.n_tiles - 1 - t

    def init(self):
        self.acc_ref[...] = jnp.zeros_like(self.acc_ref)
        self.l_ref[...] = jnp.zeros_like(self.l_ref)
        self.m_ref[...] = jnp.full(self.m_ref.shape, -jnp.inf, F32)

    def set_query_norms(self):
        qf = self.qT_ref[0:HEAD_DIM, :].astype(F32)
        self.qn_ref[...] = jnp.sqrt(jnp.sum(qf * qf, axis=0, keepdims=True))

    def diag(self):
        return [(self.d0, 0, _key_before_query(BQ, strict=False)),
                (self.d0 + 1, BK, _key_before_query(BQ - BK, strict=False))]

    def scores(self, descs):
        return [_dot(_key_block(self.k_ref, j), self.qT_ref[:, q0:]) for j, q0, _ in descs]

    def soft_group(self, descs, us):
        us = [u if mask is None else jnp.where(mask, u, -jnp.inf)
              for (_, _, mask), u in zip(descs, us)]
        mx = None
        for (_, q0, _), u in zip(descs, us):
            cm = _widen(jnp.max(u, axis=0, keepdims=True), q0, -jnp.inf)
            mx = cm if mx is None else jnp.maximum(mx, cm)
        ct = self.ct_ref[...]
        m_old = self.m_ref[...]
        m_new = jnp.maximum(m_old, mx + ct)
        off = ct - m_new
        alpha = jnp.exp(m_old - m_new)
        l = alpha * self.l_ref[...]
        acc = alpha * self.acc_ref[...]
        for (j, q0, _), u in zip(descs, us):
            p = jnp.exp(u + off[:, q0:])
            l = l + _widen(jnp.sum(p, axis=0, keepdims=True), q0, 0.0)
            acc = acc + _widen(_dot(self.vT_ref[j], p.astype(BF16)), q0, 0.0)
        self.m_ref[...] = m_new
        self.l_ref[...] = l
        self.acc_ref[...] = acc

    def batch(self, groups):
        scores = [self.scores(descs) for descs in groups]
        for descs, us in zip(groups, scores):
            self.soft_group(descs, us)

    def step(self, score_t=None, soft_t=None, pv_t=None):
        if score_t is not None:
            u = _dot(_key_block(self.k_ref, score_t[0]), self.qT_ref[...])
        if pv_t is not None:
            pv_set = self.sets[pv_t[1]]
            self.acc_ref[...] = pv_set[3][...] * self.acc_ref[...] + _dot(
                self.vT_ref[pv_t[0]], pv_set[2][...])
        if score_t is not None:
            us_ref, mxs_ref, _, _ = self.sets[score_t[1]]
            us_ref[...] = u
            mxs_ref[...] = jnp.max(u, axis=0, keepdims=True)
        if soft_t is not None:
            u_ref, mx_ref, p_ref, al_ref = self.sets[soft_t[1]]
            ct = self.ct_ref[...]
            m_old = self.m_ref[...]
            m_new = jnp.maximum(m_old, mx_ref[...] + ct)
            p = jnp.exp(u_ref[...] + (ct - m_new))
            alpha = jnp.exp(m_old - m_new)
            self.m_ref[...] = m_new
            self.l_ref[...] = alpha * self.l_ref[...] + jnp.sum(p, axis=0, keepdims=True)
            p_ref[...] = p.astype(BF16)
            al_ref[...] = alpha

    def alive(self, t):
        j = jnp.maximum(self.key_tile(t), 0)
        kmax = self.kpre_ref[self.h, j]
        c_end = self.cend_ref[self.h, j]
        bound = (self.qn_ref[...] * (kmax * FX_NORM_SLACK)
                 + (self.ct_ref[...] - c_end) - self.m_ref[...])
        return jnp.max(bound) >= -(FX_DEAD_LN + 1e-5 * jnp.abs(c_end))

    def more_after_four(self):
        return jnp.logical_and(self.alive(4), self.n_tiles > 4)

    def rest(self, enter):
        key_tile = self.key_tile

        @pl.when(enter)
        def _():
            self.step(score_t=(key_tile(4), 0))
            self.step(score_t=(key_tile(5), 1), soft_t=(key_tile(4), 0))

            def cond(carry):
                return jnp.logical_and(carry[0] < self.n_tiles, carry[1] > 0)

            def body(carry):
                t = carry[0]
                next_alive = self.alive(t + 2).astype(jnp.int32)
                self.step(score_t=(key_tile(t), 0), soft_t=(key_tile(t - 1), 1),
                          pv_t=(key_tile(t - 2), 0))
                self.step(score_t=(key_tile(t + 1), 1), soft_t=(key_tile(t), 0),
                          pv_t=(key_tile(t - 1), 1))
                return t + 2, next_alive

            t_end, _ = lax.while_loop(
                cond, body, (jnp.int32(6), self.alive(6).astype(jnp.int32)))
            self.step(soft_t=(key_tile(t_end - 1), 1), pv_t=(key_tile(t_end - 2), 0))
            self.step(pv_t=(key_tile(t_end - 1), 1))


Q_BLOCKS_PER_STEP = 4


def _fast_tiles(sb, fx):
    fx.set_query_norms()
    s_descs = sb.diag() + [(2, 0, BK, None)]
    f_groups = [fx.diag(), [(fx.key_tile(2), 0, None), (fx.key_tile(3), 0, None)]]
    s_z = sb.scores(s_descs)
    f_u = [fx.scores(descs) for descs in f_groups]
    parts = sb.splits(s_descs, s_z)
    csums = [sb.suffix_sums(*parts[0])]
    fx.soft_group(f_groups[0], f_u[0])
    csums += [sb.suffix_sums(hi, lo) for hi, lo in parts[1:]]
    fx.soft_group(f_groups[1], f_u[1])
    sb.finish(s_descs, s_z, csums)


def _remaining_tiles(sb, fx):
    late_alive = sb.alive_in(BK, BQ)
    early_alive = sb.alive_in(0, BK)
    fx_more = fx.more_after_four()

    @pl.when(late_alive)
    def _():
        sb.batch([(2, BK, BQ, None)])

    sb.rest(jnp.logical_or(late_alive, early_alive))
    fx.rest(fx_more)


def _attn_kernel(kpre_ref, cend_ref, qs_ref, qf_ref, ks_ref, kf_ref, vs_ref, vf_ref,
                 tri_ref, ct_ref, gs_ref, gf_ref, os_ref, of_ref, *scratch):
    h = pl.program_id(0)
    step = pl.program_id(1)
    sb_sets = (scratch[0:5], scratch[5:10])
    fx_sets = (scratch[10:14], scratch[14:18])
    state = scratch[18:]
    blocks = []
    for slot in range(Q_BLOCKS_PER_STEP):
        i = step * Q_BLOCKS_PER_STEP + slot
        lanes = pl.ds(slot * BQ, BQ)
        sb_acc, sb_run, fx_acc, fx_m, fx_l, fx_qn = state[6 * slot:6 * slot + 6]
        sb = _StickBreaking(i, qs_ref.at[:, lanes], ks_ref, vs_ref, tri_ref, sb_sets,
                            sb_acc, sb_run)
        fx = _Forgetting(h, i, kpre_ref, cend_ref, qf_ref.at[:, lanes], kf_ref, vf_ref,
                         ct_ref.at[:, lanes], fx_sets, fx_acc, fx_m, fx_l, fx_qn)
        sb.init()
        fx.init()
        blocks.append((sb, fx))

    @pl.when(step == 0)
    def _():
        sb0, fx0 = blocks[0]
        sb0.batch(sb0.diag())
        fx0.batch([fx0.diag()])
        for sb, fx in blocks[1:]:
            _fast_tiles(sb, fx)
            _remaining_tiles(sb, fx)

    @pl.when(step > 0)
    def _():
        for sb, fx in blocks:
            _fast_tiles(sb, fx)
            _remaining_tiles(sb, fx)

    for slot, (sb, fx) in enumerate(blocks):
        lanes = slice(slot * BQ, (slot + 1) * BQ)
        os_ref[:, lanes] = _head_rmsnorm_T(sb.acc_ref[...], gs_ref[...]).astype(os_ref.dtype)
        of_ref[:, lanes] = _head_rmsnorm_T(fx.acc_ref[...] / fx.l_ref[...],
                                           gf_ref[...]).astype(of_ref.dtype)


def _attn_scratch():
    tile_f32 = pltpu.VMEM((BK, BQ), F32)
    tile_bf16 = pltpu.VMEM((BK, BQ), BF16)
    row = pltpu.VMEM((1, BQ), F32)
    acc = pltpu.VMEM((HEAD_DIM, BQ), F32)
    sb_set = [tile_f32, tile_bf16, tile_bf16, tile_bf16, row]
    fx_set = [tile_f32, row, tile_bf16, row]
    per_block = [acc, row, acc, row, row, row]
    return sb_set + sb_set + fx_set + fx_set + per_block * Q_BLOCKS_PER_STEP


def _attention(qT, k, vT, tri, ct, kpre, cend, g_sb_col, g_fx_col):
    s = k.shape[0]
    bq = BQ * Q_BLOCKS_PER_STEP
    nkb = s // BK
    smem_spec = pl.BlockSpec(memory_space=pltpu.SMEM)
    out_spec = pl.BlockSpec((HEAD_DIM, bq), lambda h, i: (h, i))
    out_shape = jax.ShapeDtypeStruct((N_SB * HEAD_DIM, s), BF16)
    return pl.pallas_call(
        _attn_kernel,
        grid=(N_SB, s // bq),
        in_specs=[
            smem_spec, smem_spec,
            pl.BlockSpec((HEAD_PAD, bq), lambda h, i: (h, i)),
            pl.BlockSpec((HEAD_PAD, bq), lambda h, i: (N_SB + h, i)),
            pl.BlockSpec((s, HEAD_PAD), lambda h, i: (0, h // 2)),
            pl.BlockSpec((s, HEAD_PAD), lambda h, i: (0, SB_K_WIDTH // HEAD_PAD + h)),
            pl.BlockSpec((nkb, HEAD_DIM, BK), lambda h, i: (0, h, 0)),
            pl.BlockSpec((nkb, HEAD_DIM, BK), lambda h, i: (0, N_SB + h, 0)),
            pl.BlockSpec((BK, BK), lambda h, i: (0, 0)),
            pl.BlockSpec((None, 1, bq), lambda h, i: (h, 0, i)),
            pl.BlockSpec((HEAD_DIM, 1), lambda h, i: (h, 0)),
            pl.BlockSpec((HEAD_DIM, 1), lambda h, i: (h, 0)),
        ],
        out_specs=[out_spec, out_spec],
        out_shape=[out_shape, out_shape],
        scratch_shapes=_attn_scratch(),
        compiler_params=pltpu.CompilerParams(
            dimension_semantics=("arbitrary", "arbitrary"),
            vmem_limit_bytes=VMEM_LIMIT),
        name="attn",
    )(kpre, cend, qT, qT, k, k, vT, vT, tri, ct, g_sb_col, g_fx_col)


def _layer_norm(v, g, b):
    mu = jnp.mean(v, axis=-1, keepdims=True)
    vc = v - mu
    var = jnp.mean(vc * vc, axis=-1, keepdims=True)
    return vc * lax.rsqrt(var + LN_EPS) * g + b


def _ffn_kernel(alpha, x_ref, osb_ref, ofx_ref, woa_ref, wob_ref,
                g1_ref, b1_ref, g2_ref, b2_ref, wgu_ref, wd_ref,
                y_ref, act_ref):
    d_ff = wd_ref.shape[0]
    mix = (lax.dot_general(osb_ref[...], woa_ref[...], _TN, preferred_element_type=F32)
           + lax.dot_general(ofx_ref[...], wob_ref[...], _TN, preferred_element_type=F32))
    h1 = _layer_norm(alpha * x_ref[...] + mix, g1_ref[...], b1_ref[...])
    h1b = h1.astype(BF16)
    for c in range(d_ff // FF_CHUNK):
        lo, hi = c * FF_CHUNK, (c + 1) * FF_CHUNK
        gate = jnp.dot(h1b, wgu_ref[:, lo:hi], preferred_element_type=F32)
        up = jnp.dot(h1b, wgu_ref[:, d_ff + lo:d_ff + hi], preferred_element_type=F32)
        act_ref[:, lo:hi] = (gate * jax.nn.sigmoid(gate) * up).astype(BF16)
    ff = jnp.dot(act_ref[...], wd_ref[...], preferred_element_type=F32)
    y_ref[...] = _layer_norm(alpha * h1 + ff, g2_ref[...], b2_ref[...])


def _out_ffn(alpha, x2, oT_sb, oT_fx, wo_a, wo_b, g1, b1, g2, b2, wgu, wd):
    s, dm = x2.shape
    bs = BLOCK_ROWS
    d_ff = wd.shape[0]
    half = oT_sb.shape[0]
    return pl.pallas_call(
        functools.partial(_ffn_kernel, alpha),
        grid=(s // bs,),
        in_specs=[
            pl.BlockSpec((bs, dm), lambda i: (i, 0)),
            pl.BlockSpec((half, bs), lambda i: (0, i)),
            pl.BlockSpec((half, bs), lambda i: (0, i)),
            _const_spec(wo_a.shape),
            _const_spec(wo_b.shape),
            _const_spec(g1.shape),
            _const_spec(b1.shape),
            _const_spec(g2.shape),
            _const_spec(b2.shape),
            _const_spec(wgu.shape),
            _const_spec(wd.shape),
        ],
        out_specs=pl.BlockSpec((bs, dm), lambda i: (i, 0)),
        out_shape=jax.ShapeDtypeStruct((s, dm), F32),
        scratch_shapes=[pltpu.VMEM((bs, d_ff), BF16)],
        compiler_params=pltpu.CompilerParams(
            dimension_semantics=("arbitrary",), vmem_limit_bytes=VMEM_LIMIT),
        name="out_ffn",
    )(x2, oT_sb, oT_fx, wo_a, wo_b, g1, b1, g2, b2, wgu, wd)


def _layer(x2, w_in, b_f, g_sb, g_fox, w_out, ln1_g, ln1_b, ln2_g, ln2_b,
           w_gate_up, w_down, alpha):
    assert NSUB == 2
    s, dm = x2.shape
    sbw = N_SB * HEAD_DIM
    fxw = N_FX * HEAD_DIM
    q_sb, k_sb, v_sb = w_in[:, :sbw], w_in[:, sbw:2 * sbw], w_in[:, 2 * sbw:3 * sbw]
    o = 3 * sbw
    q_fx, k_fx, v_fx = w_in[:, o:o + fxw], w_in[:, o + fxw:o + 2 * fxw], w_in[:, o + 2 * fxw:o + 3 * fxw]
    wf = w_in[:, o + 3 * fxw:]

    wqT = jnp.concatenate([q_sb, q_fx], axis=1).T.astype(BF16)
    wks = k_sb.astype(BF16)
    wkf = jnp.pad(k_fx.reshape(dm, N_FX, HEAD_DIM), ((0, 0), (0, 0), (0, HEAD_PAD - HEAD_DIM)))
    wkf = wkf.reshape(dm, N_FX * HEAD_PAD).astype(BF16)
    wvT = jnp.concatenate([v_sb, v_fx], axis=1).T.astype(BF16)

    wf_hi = wf.astype(BF16)
    wf_lo = (wf - wf_hi.astype(F32)).astype(BF16)
    wf2 = jnp.concatenate([wf_hi, wf_lo], axis=1)

    qT, k, vT, c, kpre, cend = _projection(x2, wqT, wks, wkf, wvT, wf2,
                                           b_f.reshape(1, N_FX))

    srow = lax.broadcasted_iota(jnp.int32, (BK, BK), 0)
    jcol = lax.broadcasted_iota(jnp.int32, (BK, BK), 1)
    tri = (jcol >= srow).astype(BF16)
    ct = c.T.reshape(N_FX, 1, s)
    kpre = kpre.reshape(s // BK, N_FX).T
    cend = cend.reshape(s // BK, N_FX).T

    oT_sb, oT_fx = _attention(qT, k, vT, tri, ct, kpre, cend,
                              g_sb.reshape(sbw, 1), g_fox.reshape(fxw, 1))

    wo = w_out.astype(BF16)
    return _out_ffn(
        alpha, x2, oT_sb, oT_fx, wo[:sbw], wo[sbw:],
        ln1_g.reshape(1, dm), ln1_b.reshape(1, dm),
        ln2_g.reshape(1, dm), ln2_b.reshape(1, dm),
        w_gate_up.astype(BF16), w_down.astype(BF16))


def kernel(x, w_in, b_f, g_sb, g_fox, w_out, ln1_g, ln1_b, ln2_g, ln2_b, w_gate_up, w_down):
    batch, s, dm = x.shape
    depth = w_in.shape[0]
    alpha = (2 * depth) ** 0.25
    outs = []
    for b in range(batch):
        h = x[b]
        for l in range(depth):
            h = _layer(h, w_in[l], b_f[l], g_sb[l], g_fox[l], w_out[l],
                       ln1_g[l], ln1_b[l], ln2_g[l], ln2_b[l],
                       w_gate_up[l], w_down[l], alpha)
        outs.append(h)
    return outs[0][None] if batch == 1 else jnp.stack(outs, axis=0)
```

```python
import functools

import jax
import jax.numpy as jnp
from jax import lax
from jax.experimental import pallas as pl
from jax.experimental.pallas import tpu as pltpu

F32 = jnp.float32
BF16 = jnp.bfloat16

HEAD_DIM = 64
HEAD_PAD = 128
N_SB = 8
N_FX = 8
N_HEADS = N_SB + N_FX
SB_K_WIDTH = N_SB * HEAD_DIM
K_WIDTH = SB_K_WIDTH + N_FX * HEAD_PAD
LN_EPS = 1e-5
RMS_EPS = 1e-6
LOG2E = 1.4426950408889634

BLOCK_ROWS = 512
BQ = 512
BK = 256
NSUB = BQ // BK
FF_CHUNK = 256
VMEM_LIMIT = 56 * 1024 * 1024

SB_DEAD_LOG2 = 160.0
FX_DEAD_LN = 110.0
FX_NORM_SLACK = 1.001

_NT = (((1,), (1,)), ((), ()))
_TN = (((0,), (0,)), ((), ()))


def _const_spec(shape):
    nd = len(shape)
    return pl.BlockSpec(shape, lambda *_: (0,) * nd, pipeline_mode=pl.Buffered(1))


def _proj_kernel(x_ref, wqT_ref, wks_ref, wkf_ref, wvT_ref, wf_ref, bf_ref,
                 qT_ref, k_ref, vT_ref, c_ref, kpre_ref, cend_ref,
                 carry_ref, kmax_ref):
    i = pl.program_id(0)
    bs = x_ref.shape[0]
    n_kt = bs // BK

    @pl.when(i == 0)
    def _():
        carry_ref[...] = jnp.zeros_like(carry_ref)
        kmax_ref[...] = jnp.zeros_like(kmax_ref)

    xb = x_ref[...]
    xbf = xb.astype(BF16)

    x_lo = (xb - xbf.astype(F32)).astype(BF16)
    wf2 = wf_ref[...]
    f_hi = jnp.dot(xbf, wf2, preferred_element_type=F32)
    f_lo = jnp.dot(x_lo, wf2[:, :N_FX], preferred_element_type=F32)
    f = f_hi[:, :N_FX] + f_hi[:, N_FX:] + f_lo + bf_ref[...]
    logf = jnp.minimum(f, 0.0) - jnp.log1p(jnp.exp(-jnp.abs(f)))
    row = lax.broadcasted_iota(jnp.int32, logf.shape, 0)
    c = logf
    d = 1
    while d < bs:
        c = c + jnp.where(row >= d, pltpu.roll(c, d, axis=0), 0.0)
        d *= 2
    c = c + carry_ref[...]
    carry_ref[...] = c[bs - 1:bs, :]
    c_ref[...] = c
    stat_tile = lax.broadcasted_iota(jnp.int32, (n_kt, N_FX), 0)
    stat_head = lax.broadcasted_iota(jnp.int32, (n_kt, N_FX), 1)
    c_end = jnp.zeros((n_kt, N_FX), F32)
    for t in range(n_kt):
        last = jnp.broadcast_to(c[(t + 1) * BK - 1:(t + 1) * BK, :], (n_kt, N_FX))
        c_end = jnp.where(stat_tile == t, last, c_end)
    cend_ref[0] = c_end

    for cidx in range(SB_K_WIDTH // 256):
        cols = slice(cidx * 256, (cidx + 1) * 256)
        k_ref[:, cols] = jnp.dot(xbf, wks_ref[:, cols],
                                 preferred_element_type=F32).astype(BF16)

    lane = lax.broadcasted_iota(jnp.int32, (bs, HEAD_PAD), 1)
    knorm2 = jnp.zeros((n_kt, N_FX), F32)
    for hp in range(N_FX // 2):
        r = jnp.dot(xbf, wkf_ref[:, hp * 256:(hp + 1) * 256],
                    preferred_element_type=F32)
        for sub in range(2):
            hh = 2 * hp + sub
            rh = r[:, sub * HEAD_PAD:(sub + 1) * HEAD_PAD]
            kr = rh.astype(BF16).astype(F32)
            n2 = jnp.sum(kr * kr, axis=1, keepdims=True)
            for t in range(n_kt):
                tmax = jnp.max(n2[t * BK:(t + 1) * BK, :], axis=0, keepdims=True)
                knorm2 = jnp.where((stat_tile == t) & (stat_head == hh),
                                   jnp.broadcast_to(tmax, (n_kt, N_FX)), knorm2)
            cb = jnp.broadcast_to(c[:, hh:hh + 1], (bs, HEAD_PAD))
            hi = cb.astype(BF16).astype(F32)
            r1 = cb - hi
            mid = r1.astype(BF16).astype(F32)
            lo = (r1 - mid).astype(BF16).astype(F32)
            aug = jnp.where(lane == HEAD_DIM, hi,
                            jnp.where(lane == HEAD_DIM + 1, mid,
                                      jnp.where(lane == HEAD_DIM + 2, lo, 0.0)))
            base = SB_K_WIDTH + hh * HEAD_PAD
            k_ref[:, base:base + HEAD_PAD] = (rh + aug).astype(BF16)

    knorm = jnp.sqrt(knorm2)
    k_pre = jnp.zeros((n_kt, N_FX), F32)
    run_max = kmax_ref[...]
    for t in range(n_kt):
        run_max = jnp.maximum(run_max, knorm[t:t + 1, :])
        k_pre = jnp.where(stat_tile == t, jnp.broadcast_to(run_max, (n_kt, N_FX)), k_pre)
    kmax_ref[...] = run_max
    kpre_ref[0] = k_pre

    pad_rows = HEAD_PAD - HEAD_DIM
    prow = lax.broadcasted_iota(jnp.int32, (pad_rows, bs), 0)
    neg_rows = jnp.where(prow < 3, -1.0, 0.0).astype(BF16)
    zero_rows = jnp.zeros((pad_rows, bs), BF16)
    for cidx in range(4):
        r = lax.dot_general(wqT_ref[cidx * 256:(cidx + 1) * 256, :], xbf, _NT,
                            preferred_element_type=F32)
        for sub in range(4):
            h = cidx * 4 + sub
            base = h * HEAD_PAD
            q_rows = r[sub * HEAD_DIM:(sub + 1) * HEAD_DIM, :]
            if h < N_SB:
                own = base + (h % 2) * HEAD_DIM
                other = base + (1 - h % 2) * HEAD_DIM
                qT_ref[own:own + HEAD_DIM, :] = (
                    q_rows * (HEAD_DIM ** -0.5 * LOG2E)).astype(BF16)
                qT_ref[other:other + HEAD_DIM, :] = zero_rows
            else:
                qT_ref[base:base + HEAD_DIM, :] = (q_rows * HEAD_DIM ** -0.5).astype(BF16)
                qT_ref[base + HEAD_DIM:base + HEAD_PAD, :] = neg_rows

    for cidx in range(4):
        r = lax.dot_general(wvT_ref[cidx * 256:(cidx + 1) * 256, :], xbf, _NT,
                            preferred_element_type=F32)
        for jb in range(n_kt):
            vT_ref[jb, cidx * 256:(cidx + 1) * 256, :] = (
                r[:, jb * BK:(jb + 1) * BK].astype(BF16))


def _projection(x2, wqT, wks, wkf, wvT, wf, bf):
    s, dm = x2.shape
    bs = BLOCK_ROWS
    n_kt = bs // BK
    return pl.pallas_call(
        _proj_kernel,
        grid=(s // bs,),
        in_specs=[
            pl.BlockSpec((bs, dm), lambda i: (i, 0)),
            _const_spec(wqT.shape),
            _const_spec(wks.shape),
            _const_spec(wkf.shape),
            _const_spec(wvT.shape),
            _const_spec(wf.shape),
            _const_spec(bf.shape),
        ],
        out_specs=[
            pl.BlockSpec((N_HEADS * HEAD_PAD, bs), lambda i: (0, i)),
            pl.BlockSpec((bs, K_WIDTH), lambda i: (i, 0)),
            pl.BlockSpec((n_kt, N_HEADS * HEAD_DIM, BK), lambda i: (i, 0, 0)),
            pl.BlockSpec((bs, N_FX), lambda i: (i, 0)),
            pl.BlockSpec((1, n_kt, N_FX), lambda i: (i, 0, 0)),
            pl.BlockSpec((1, n_kt, N_FX), lambda i: (i, 0, 0)),
        ],
        out_shape=[
            jax.ShapeDtypeStruct((N_HEADS * HEAD_PAD, s), BF16),
            jax.ShapeDtypeStruct((s, K_WIDTH), BF16),
            jax.ShapeDtypeStruct((s // BK, N_HEADS * HEAD_DIM, BK), BF16),
            jax.ShapeDtypeStruct((s, N_FX), F32),
            jax.ShapeDtypeStruct((s // bs, n_kt, N_FX), F32),
            jax.ShapeDtypeStruct((s // bs, n_kt, N_FX), F32),
        ],
        scratch_shapes=[pltpu.VMEM((1, N_FX), F32), pltpu.VMEM((1, N_FX), F32)],
        compiler_params=pltpu.CompilerParams(
            dimension_semantics=("arbitrary",), vmem_limit_bytes=VMEM_LIMIT),
        name="proj",
    )(x2, wqT, wks, wkf, wvT, wf, bf)


def _head_rmsnorm_T(oT, g_col):
    ms = jnp.mean(oT * oT, axis=0, keepdims=True)
    return oT * lax.rsqrt(ms + RMS_EPS) * g_col


def _key_before_query(n_queries, strict):
    srow = lax.broadcasted_iota(jnp.int32, (BK, n_queries), 0)
    tcol = lax.broadcasted_iota(jnp.int32, (BK, n_queries), 1)
    return srow < tcol if strict else srow <= tcol


def _key_block(k_ref, j):
    return k_ref[pl.ds(pl.multiple_of(j * BK, BK), BK), :]


def _dot(a, b):
    return jnp.dot(a, b, preferred_element_type=F32)


def _place(x, q_lo, q_hi, fill):
    pieces = []
    if q_lo:
        pieces.append(jnp.full((x.shape[0], q_lo), fill, F32))
    pieces.append(x)
    if q_hi < BQ:
        pieces.append(jnp.full((x.shape[0], BQ - q_hi), fill, F32))
    return pieces[0] if len(pieces) == 1 else jnp.concatenate(pieces, axis=1)


def _widen(x, q0, fill):
    return _place(x, q0, BQ, fill)


class _StickBreaking:
    def __init__(self, i, qT_ref, k_ref, vT_ref, tri_ref, sets, acc_ref, run_ref):
        self.qT_ref, self.k_ref, self.vT_ref, self.tri_ref = qT_ref, k_ref, vT_ref, tri_ref
        self.sets, self.acc_ref, self.run_ref = sets, acc_ref, run_ref
        self.n_tiles = NSUB * (i + 1)

    def key_tile(self, t):
        return self.n_tiles - 1 - t

    def init(self):
        self.acc_ref[...] = jnp.zeros_like(self.acc_ref)
        self.run_ref[...] = jnp.zeros_like(self.run_ref)

    def diag(self):
        return [(0, BK, BQ, _key_before_query(BQ - BK, strict=True)),
                (1, 0, BQ, _key_before_query(BQ, strict=True))]

    @staticmethod
    def softplus2(z):
        return jnp.maximum(z, 0.0) + jnp.log(1.0 + jnp.exp2(-jnp.abs(z))) * LOG2E

    @staticmethod
    def split(sp):
        hi = sp.astype(BF16)
        return hi, (sp - hi.astype(F32)).astype(BF16)

    def suffix_sums(self, hi, lo):
        tri = self.tri_ref[...]
        return _dot(tri, hi) + _dot(tri, lo)

    def scores(self, descs):
        return [_dot(_key_block(self.k_ref, self.key_tile(t)), self.qT_ref[:, q_lo:q_hi])
                for t, q_lo, q_hi, _ in descs]

    def splits(self, descs, zs):
        parts = []
        for desc, z in zip(descs, zs):
            sp = self.softplus2(z)
            if desc[3] is not None:
                sp = jnp.where(desc[3], sp, 0.0)
            parts.append(self.split(sp))
        return parts

    def finish(self, descs, zs, csums):
        ws = []
        for desc, z, csum in zip(descs, zs, csums):
            w = jnp.exp2(z - csum)
            if desc[3] is not None:
                w = jnp.where(desc[3], w, 0.0)
            ws.append(w.astype(BF16))
        pvs = [_dot(self.vT_ref[self.key_tile(desc[0])], w) for desc, w in zip(descs, ws)]
        run = self.run_ref[...]
        acc = self.acc_ref[...]
        for (_, q_lo, q_hi, _), csum, pv in zip(descs, csums, pvs):
            acc = acc + _place(pv, q_lo, q_hi, 0.0) * jnp.exp2(-run)
            run = run + _place(csum[0:1, :], q_lo, q_hi, 0.0)
        self.acc_ref[...] = acc
        self.run_ref[...] = run

    def batch(self, descs):
        zs = self.scores(descs)
        parts = self.splits(descs, zs)
        self.finish(descs, zs, [self.suffix_sums(hi, lo) for hi, lo in parts])

    def step(self, score_t=None, csum_t=None, pv_t=None):
        if score_t is not None:
            z = _dot(_key_block(self.k_ref, self.key_tile(score_t[0])), self.qT_ref[...])
        if csum_t is not None:
            z_ref, hi_ref, lo_ref, w_ref, r_ref = self.sets[csum_t[1]]
            csum = self.suffix_sums(hi_ref[...], lo_ref[...])
        if pv_t is not None:
            pv_set = self.sets[pv_t[1]]
            self.acc_ref[...] += pv_set[4][...] * _dot(
                self.vT_ref[self.key_tile(pv_t[0])], pv_set[3][...])
        if score_t is not None:
            zs_ref, his_ref, los_ref, _, _ = self.sets[score_t[1]]
            hi, lo = self.split(self.softplus2(z))
            zs_ref[...] = z
            his_ref[...] = hi
            los_ref[...] = lo
        if csum_t is not None:
            w_ref[...] = jnp.exp2(z_ref[...] - csum).astype(BF16)
            run = self.run_ref[...]
            r_ref[...] = jnp.exp2(-run)
            self.run_ref[...] = run + csum[0:1, :]

    def alive_in(self, q_lo, q_hi):
        return jnp.min(self.run_ref[:, q_lo:q_hi]) < SB_DEAD_LOG2

    def alive(self):
        return self.alive_in(0, BQ)

    def rest(self, enter):
        @pl.when(enter)
        def _():
            self._rest_if_alive()

    def _rest_if_alive(self):
        @pl.when(self.alive())
        def _():
            self.batch([(3, 0, BQ, None)])

            @pl.when(jnp.logical_and(self.alive(), self.n_tiles > 4))
            def _():
                self.step(score_t=(4, 0))
                self.step(score_t=(5, 1), csum_t=(4, 0))

                def cond(carry):
                    return jnp.logical_and(carry[0] < self.n_tiles, carry[1] > 0)

                def body(carry):
                    t = carry[0]
                    still_alive = self.alive().astype(jnp.int32)
                    self.step(score_t=(t, 0), csum_t=(t - 1, 1), pv_t=(t - 2, 0))
                    self.step(score_t=(t + 1, 1), csum_t=(t, 0), pv_t=(t - 1, 1))
                    return t + 2, still_alive

                t_end, _ = lax.while_loop(cond, body, (jnp.int32(6), jnp.int32(1)))
                self.step(csum_t=(t_end - 1, 1), pv_t=(t_end - 2, 0))
                self.step(pv_t=(t_end - 1, 1))


class _Forgetting:
    def __init__(self, h, i, kpre_ref, cend_ref, qT_ref, k_ref, vT_ref, ct_ref, sets,
                 acc_ref, m_ref, l_ref, qn_ref):
        self.h, self.kpre_ref, self.cend_ref = h, kpre_ref, cend_ref
        self.qT_ref, self.k_ref, self.vT_ref, self.ct_ref = qT_ref, k_ref, vT_ref, ct_ref
        self.sets, self.acc_ref, self.m_ref, self.l_ref, self.qn_ref = (
            sets, acc_ref, m_ref, l_ref, qn_ref)
        self.n_tiles = NSUB * (i + 1)
        self.d0 = NSUB * i

    def key_tile(self, t):
        return self.n_tiles - 1 - t

    def init(self):
        self.acc_ref[...] = jnp.zeros_like(self.acc_ref)
        self.l_ref[...] = jnp.zeros_like(self.l_ref)
        self.m_ref[...] = jnp.full(self.m_ref.shape, -jnp.inf, F32)

    def set_query_norms(self):
        qf = self.qT_ref[0:HEAD_DIM, :].astype(F32)
        self.qn_ref[...] = jnp.sqrt(jnp.sum(qf * qf, axis=0, keepdims=True))

    def diag(self):
        return [(self.d0, 0, _key_before_query(BQ, strict=False)),
                (self.d0 + 1, BK, _key_before_query(BQ - BK, strict=False))]

    def scores(self, descs):
        return [_dot(_key_block(self.k_ref, j), self.qT_ref[:, q0:]) for j, q0, _ in descs]

    def soft_group(self, descs, us):
        us = [u if mask is None else jnp.where(mask, u, -jnp.inf)
              for (_, _, mask), u in zip(descs, us)]
        mx = None
        for (_, q0, _), u in zip(descs, us):
            cm = _widen(jnp.max(u, axis=0, keepdims=True), q0, -jnp.inf)
            mx = cm if mx is None else jnp.maximum(mx, cm)
        ct = self.ct_ref[...]
        m_old = self.m_ref[...]
        m_new = jnp.maximum(m_old, mx + ct)
        off = ct - m_new
        alpha = jnp.exp(m_old - m_new)
        l = alpha * self.l_ref[...]
        acc = alpha * self.acc_ref[...]
        for (j, q0, _), u in zip(descs, us):
            p = jnp.exp(u + off[:, q0:])
            l = l + _widen(jnp.sum(p, axis=0, keepdims=True), q0, 0.0)
            acc = acc + _widen(_dot(self.vT_ref[j], p.astype(BF16)), q0, 0.0)
        self.m_ref[...] = m_new
        self.l_ref[...] = l
        self.acc_ref[...] = acc

    def batch(self, groups):
        scores = [self.scores(descs) for descs in groups]
        for descs, us in zip(groups, scores):
            self.soft_group(descs, us)

    def step(self, score_t=None, soft_t=None, pv_t=None):
        if score_t is not None:
            u = _dot(_key_block(self.k_ref, score_t[0]), self.qT_ref[...])
        if pv_t is not None:
            pv_set = self.sets[pv_t[1]]
            self.acc_ref[...] = pv_set[3][...] * self.acc_ref[...] + _dot(
                self.vT_ref[pv_t[0]], pv_set[2][...])
        if score_t is not None:
            us_ref, mxs_ref, _, _ = self.sets[score_t[1]]
            us_ref[...] = u
            mxs_ref[...] = jnp.max(u, axis=0, keepdims=True)
        if soft_t is not None:
            u_ref, mx_ref, p_ref, al_ref = self.sets[soft_t[1]]
            ct = self.ct_ref[...]
            m_old = self.m_ref[...]
            m_new = jnp.maximum(m_old, mx_ref[...] + ct)
            p = jnp.exp(u_ref[...] + (ct - m_new))
            alpha = jnp.exp(m_old - m_new)
            self.m_ref[...] = m_new
            self.l_ref[...] = alpha * self.l_ref[...] + jnp.sum(p, axis=0, keepdims=True)
            p_ref[...] = p.astype(BF16)
            al_ref[...] = alpha

    def alive(self, t):
        j = jnp.maximum(self.key_tile(t), 0)
        kmax = self.kpre_ref[self.h, j]
        c_end = self.cend_ref[self.h, j]
        bound = (self.qn_ref[...] * (kmax * FX_NORM_SLACK)
                 + (self.ct_ref[...] - c_end) - self.m_ref[...])
        return jnp.max(bound) >= -(FX_DEAD_LN + 1e-5 * jnp.abs(c_end))

    def more_after_four(self):
        return jnp.logical_and(self.alive(4), self.n_tiles > 4)

    def rest(self, enter):
        key_tile = self.key_tile

        @pl.when(enter)
        def _():
            self.step(score_t=(key_tile(4), 0))
            self.step(score_t=(key_tile(5), 1), soft_t=(key_tile(4), 0))

            def cond(carry):
                return jnp.logical_and(carry[0] < self.n_tiles, carry[1] > 0)

            def body(carry):
                t = carry[0]
                next_alive = self.alive(t + 2).astype(jnp.int32)
                self.step(score_t=(key_tile(t), 0), soft_t=(key_tile(t - 1), 1),
                          pv_t=(key_tile(t - 2), 0))
                self.step(score_t=(key_tile(t + 1), 1), soft_t=(key_tile(t), 0),
                          pv_t=(key_tile(t - 1), 1))
                return t + 2, next_alive

            t_end, _ = lax.while_loop(
                cond, body, (jnp.int32(6), self.alive(6).astype(jnp.int32)))
            self.step(soft_t=(key_tile(t_end - 1), 1), pv_t=(key_tile(t_end - 2), 0))
            self.step(pv_t=(key_tile(t_end - 1), 1))


Q_BLOCKS_PER_STEP = 2


def _fast_tiles(sb, fx):
    fx.set_query_norms()
    s_descs = sb.diag() + [(2, 0, BK, None)]
    f_groups = [fx.diag(), [(fx.key_tile(2), 0, None), (fx.key_tile(3), 0, None)]]
    s_z = sb.scores(s_descs)
    f_u = [fx.scores(descs) for descs in f_groups]
    parts = sb.splits(s_descs, s_z)
    csums = [sb.suffix_sums(*parts[0])]
    fx.soft_group(f_groups[0], f_u[0])
    csums += [sb.suffix_sums(hi, lo) for hi, lo in parts[1:]]
    fx.soft_group(f_groups[1], f_u[1])
    sb.finish(s_descs, s_z, csums)


def _remaining_tiles(sb, fx):
    late_alive = sb.alive_in(BK, BQ)
    early_alive = sb.alive_in(0, BK)
    fx_more = fx.more_after_four()

    @pl.when(late_alive)
    def _():
        sb.batch([(2, BK, BQ, None)])

    sb.rest(jnp.logical_or(late_alive, early_alive))
    fx.rest(fx_more)


def _attn_kernel(kpre_ref, cend_ref, qs_ref, qf_ref, ks_ref, kf_ref, vs_ref, vf_ref,
                 tri_ref, ct_ref, gs_ref, gf_ref, os_ref, of_ref, *scratch):
    h = pl.program_id(0)
    step = pl.program_id(1)
    sb_sets = (scratch[0:5], scratch[5:10])
    fx_sets = (scratch[10:14], scratch[14:18])
    state = scratch[18:]
    blocks = []
    for slot in range(Q_BLOCKS_PER_STEP):
        i = step * Q_BLOCKS_PER_STEP + slot
        lanes = pl.ds(slot * BQ, BQ)
        sb_acc, sb_run, fx_acc, fx_m, fx_l, fx_qn = state[6 * slot:6 * slot + 6]
        sb = _StickBreaking(i, qs_ref.at[:, lanes], ks_ref, vs_ref, tri_ref, sb_sets,
                            sb_acc, sb_run)
        fx = _Forgetting(h, i, kpre_ref, cend_ref, qf_ref.at[:, lanes], kf_ref, vf_ref,
                         ct_ref.at[:, lanes], fx_sets, fx_acc, fx_m, fx_l, fx_qn)
        sb.init()
        fx.init()
        blocks.append((sb, fx))

    @pl.when(step == 0)
    def _():
        sb0, fx0 = blocks[0]
        sb0.batch(sb0.diag())
        fx0.batch([fx0.diag()])
        for sb, fx in blocks[1:]:
            _fast_tiles(sb, fx)
            _remaining_tiles(sb, fx)

    @pl.when(step > 0)
    def _():
        for sb, fx in blocks:
            _fast_tiles(sb, fx)
            _remaining_tiles(sb, fx)

    for slot, (sb, fx) in enumerate(blocks):
        lanes = slice(slot * BQ, (slot + 1) * BQ)
        os_ref[:, lanes] = _head_rmsnorm_T(sb.acc_ref[...], gs_ref[...]).astype(os_ref.dtype)
        of_ref[:, lanes] = _head_rmsnorm_T(fx.acc_ref[...] / fx.l_ref[...],
                                           gf_ref[...]).astype(of_ref.dtype)


def _attn_scratch():
    tile_f32 = pltpu.VMEM((BK, BQ), F32)
    tile_bf16 = pltpu.VMEM((BK, BQ), BF16)
    row = pltpu.VMEM((1, BQ), F32)
    acc = pltpu.VMEM((HEAD_DIM, BQ), F32)
    sb_set = [tile_f32, tile_bf16, tile_bf16, tile_bf16, row]
    fx_set = [tile_f32, row, tile_bf16, row]
    per_block = [acc, row, acc, row, row, row]
    return sb_set + sb_set + fx_set + fx_set + per_block * Q_BLOCKS_PER_STEP


def _attention(qT, k, vT, tri, ct, kpre, cend, g_sb_col, g_fx_col):
    s = k.shape[0]
    bq = BQ * Q_BLOCKS_PER_STEP
    nkb = s // BK
    smem_spec = pl.BlockSpec(memory_space=pltpu.SMEM)
    out_spec = pl.BlockSpec((HEAD_DIM, bq), lambda h, i: (h, i))
    out_shape = jax.ShapeDtypeStruct((N_SB * HEAD_DIM, s), BF16)
    return pl.pallas_call(
        _attn_kernel,
        grid=(N_SB, s // bq),
        in_specs=[
            smem_spec, smem_spec,
            pl.BlockSpec((HEAD_PAD, bq), lambda h, i: (h, i)),
            pl.BlockSpec((HEAD_PAD, bq), lambda h, i: (N_SB + h, i)),
            pl.BlockSpec((s, HEAD_PAD), lambda h, i: (0, h // 2)),
            pl.BlockSpec((s, HEAD_PAD), lambda h, i: (0, SB_K_WIDTH // HEAD_PAD + h)),
            pl.BlockSpec((nkb, HEAD_DIM, BK), lambda h, i: (0, h, 0)),
            pl.BlockSpec((nkb, HEAD_DIM, BK), lambda h, i: (0, N_SB + h, 0)),
            pl.BlockSpec((BK, BK), lambda h, i: (0, 0)),
            pl.BlockSpec((None, 1, bq), lambda h, i: (h, 0, i)),
            pl.BlockSpec((HEAD_DIM, 1), lambda h, i: (h, 0)),
            pl.BlockSpec((HEAD_DIM, 1), lambda h, i: (h, 0)),
        ],
        out_specs=[out_spec, out_spec],
        out_shape=[out_shape, out_shape],
        scratch_shapes=_attn_scratch(),
        compiler_params=pltpu.CompilerParams(
            dimension_semantics=("arbitrary", "arbitrary"),
            vmem_limit_bytes=VMEM_LIMIT),
        name="attn",
    )(kpre, cend, qT, qT, k, k, vT, vT, tri, ct, g_sb_col, g_fx_col)


def _layer_norm(v, g, b):
    mu = jnp.mean(v, axis=-1, keepdims=True)
    vc = v - mu
    var = jnp.mean(vc * vc, axis=-1, keepdims=True)
    return vc * lax.rsqrt(var + LN_EPS) * g + b


def _ffn_kernel(alpha, x_ref, osb_ref, ofx_ref, woa_ref, wob_ref,
                g1_ref, b1_ref, g2_ref, b2_ref, wgu_ref, wd_ref,
                y_ref, act_ref):
    d_ff = wd_ref.shape[0]
    mix = (lax.dot_general(osb_ref[...], woa_ref[...], _TN, preferred_element_type=F32)
           + lax.dot_general(ofx_ref[...], wob_ref[...], _TN, preferred_element_type=F32))
    h1 = _layer_norm(alpha * x_ref[...] + mix, g1_ref[...], b1_ref[...])
    h1b = h1.astype(BF16)
    for c in range(d_ff // FF_CHUNK):
        lo, hi = c * FF_CHUNK, (c + 1) * FF_CHUNK
        gate = jnp.dot(h1b, wgu_ref[:, lo:hi], preferred_element_type=F32)
        up = jnp.dot(h1b, wgu_ref[:, d_ff + lo:d_ff + hi], preferred_element_type=F32)
        act_ref[:, lo:hi] = (gate * jax.nn.sigmoid(gate) * up).astype(BF16)
    ff = jnp.dot(act_ref[...], wd_ref[...], preferred_element_type=F32)
    y_ref[...] = _layer_norm(alpha * h1 + ff, g2_ref[...], b2_ref[...])


def _out_ffn(alpha, x2, oT_sb, oT_fx, wo_a, wo_b, g1, b1, g2, b2, wgu, wd):
    s, dm = x2.shape
    bs = BLOCK_ROWS
    d_ff = wd.shape[0]
    half = oT_sb.shape[0]
    return pl.pallas_call(
        functools.partial(_ffn_kernel, alpha),
        grid=(s // bs,),
        in_specs=[
            pl.BlockSpec((bs, dm), lambda i: (i, 0)),
            pl.BlockSpec((half, bs), lambda i: (0, i)),
            pl.BlockSpec((half, bs), lambda i: (0, i)),
            _const_spec(wo_a.shape),
            _const_spec(wo_b.shape),
            _const_spec(g1.shape),
            _const_spec(b1.shape),
            _const_spec(g2.shape),
            _const_spec(b2.shape),
            _const_spec(wgu.shape),
            _const_spec(wd.shape),
        ],
        out_specs=pl.BlockSpec((bs, dm), lambda i: (i, 0)),
        out_shape=jax.ShapeDtypeStruct((s, dm), F32),
        scratch_shapes=[pltpu.VMEM((bs, d_ff), BF16)],
        compiler_params=pltpu.CompilerParams(
            dimension_semantics=("arbitrary",), vmem_limit_bytes=VMEM_LIMIT),
        name="out_ffn",
    )(x2, oT_sb, oT_fx, wo_a, wo_b, g1, b1, g2, b2, wgu, wd)


def _layer(x2, w_in, b_f, g_sb, g_fox, w_out, ln1_g, ln1_b, ln2_g, ln2_b,
           w_gate_up, w_down, alpha):
    assert NSUB == 2
    s, dm = x2.shape
    sbw = N_SB * HEAD_DIM
    fxw = N_FX * HEAD_DIM
    q_sb, k_sb, v_sb = w_in[:, :sbw], w_in[:, sbw:2 * sbw], w_in[:, 2 * sbw:3 * sbw]
    o = 3 * sbw
    q_fx, k_fx, v_fx = w_in[:, o:o + fxw], w_in[:, o + fxw:o + 2 * fxw], w_in[:, o + 2 * fxw:o + 3 * fxw]
    wf = w_in[:, o + 3 * fxw:]

    wqT = jnp.concatenate([q_sb, q_fx], axis=1).T.astype(BF16)
    wks = k_sb.astype(BF16)
    wkf = jnp.pad(k_fx.reshape(dm, N_FX, HEAD_DIM), ((0, 0), (0, 0), (0, HEAD_PAD - HEAD_DIM)))
    wkf = wkf.reshape(dm, N_FX * HEAD_PAD).astype(BF16)
    wvT = jnp.concatenate([v_sb, v_fx], axis=1).T.astype(BF16)

    wf_hi = wf.astype(BF16)
    wf_lo = (wf - wf_hi.astype(F32)).astype(BF16)
    wf2 = jnp.concatenate([wf_hi, wf_lo], axis=1)

    qT, k, vT, c, kpre, cend = _projection(x2, wqT, wks, wkf, wvT, wf2,
                                           b_f.reshape(1, N_FX))

    srow = lax.broadcasted_iota(jnp.int32, (BK, BK), 0)
    jcol = lax.broadcasted_iota(jnp.int32, (BK, BK), 1)
    tri = (jcol >= srow).astype(BF16)
    ct = c.T.reshape(N_FX, 1, s)
    kpre = kpre.reshape(s // BK, N_FX).T
    cend = cend.reshape(s // BK, N_FX).T

    oT_sb, oT_fx = _attention(qT, k, vT, tri, ct, kpre, cend,
                              g_sb.reshape(sbw, 1), g_fox.reshape(fxw, 1))

    wo = w_out.astype(BF16)
    return _out_ffn(
        alpha, x2, oT_sb, oT_fx, wo[:sbw], wo[sbw:],
        ln1_g.reshape(1, dm), ln1_b.reshape(1, dm),
        ln2_g.reshape(1, dm), ln2_b.reshape(1, dm),
        w_gate_up.astype(BF16), w_down.astype(BF16))


def kernel(x, w_in, b_f, g_sb, g_fox, w_out, ln1_g, ln1_b, ln2_g, ln2_b, w_gate_up, w_down):
    batch, s, dm = x.shape
    depth = w_in.shape[0]
    alpha = (2 * depth) ** 0.25
    outs = []
    for b in range(batch):
        h = x[b]
        for l in range(depth):
            h = _layer(h, w_in[l], b_f[l], g_sb[l], g_fox[l], w_out[l],
                       ln1_g[l], ln1_b[l], ln2_g[l], ln2_b[l],
                       w_gate_up[l], w_down[l], alpha)
        outs.append(h)
    return outs[0][None] if batch == 1 else jnp.stack(outs, axis=0)
```

```python
import functools

import jax
import jax.numpy as jnp
from jax import lax
from jax.experimental import pallas as pl
from jax.experimental.pallas import tpu as pltpu

F32 = jnp.float32
BF16 = jnp.bfloat16

HEAD_DIM = 64
HEAD_PAD = 128
N_SB = 8
N_FX = 8
N_HEADS = N_SB + N_FX
SB_K_WIDTH = N_SB * HEAD_DIM
K_WIDTH = SB_K_WIDTH + N_FX * HEAD_PAD
LN_EPS = 1e-5
RMS_EPS = 1e-6
LOG2E = 1.4426950408889634

BLOCK_ROWS = 512
BQ = 512
BK = 256
NSUB = BQ // BK
FF_CHUNK = 256
VMEM_LIMIT = 56 * 1024 * 1024

SB_DEAD_LOG2 = 160.0
FX_DEAD_LOG2 = 160.0
FX_NORM_SLACK = 1.001

_NT = (((1,), (1,)), ((), ()))
_TN = (((0,), (0,)), ((), ()))


def _const_spec(shape):
    nd = len(shape)
    return pl.BlockSpec(shape, lambda *_: (0,) * nd, pipeline_mode=pl.Buffered(1))


def _proj_kernel(x_ref, wqT_ref, wks_ref, wkf_ref, wvT_ref, wf_ref, bf_ref,
                 qT_ref, k_ref, vT_ref, c_ref, kpre_ref, cend_ref,
                 carry_ref, kmax_ref):
    i = pl.program_id(0)
    bs = x_ref.shape[0]
    n_kt = bs // BK

    @pl.when(i == 0)
    def _():
        carry_ref[...] = jnp.zeros_like(carry_ref)
        kmax_ref[...] = jnp.zeros_like(kmax_ref)

    xb = x_ref[...]
    xbf = xb.astype(BF16)

    x_lo = (xb - xbf.astype(F32)).astype(BF16)
    wf2 = wf_ref[...]
    f_hi = jnp.dot(xbf, wf2, preferred_element_type=F32)
    f_lo = jnp.dot(x_lo, wf2[:, :N_FX], preferred_element_type=F32)
    f = f_hi[:, :N_FX] + f_hi[:, N_FX:] + f_lo + bf_ref[...]
    logf = jnp.minimum(f, 0.0) - jnp.log1p(jnp.exp(-jnp.abs(f)))
    row = lax.broadcasted_iota(jnp.int32, logf.shape, 0)
    c = logf
    d = 1
    while d < bs:
        c = c + jnp.where(row >= d, pltpu.roll(c, d, axis=0), 0.0)
        d *= 2
    c = c + carry_ref[...]
    carry_ref[...] = c[bs - 1:bs, :]
    c = c * LOG2E
    c_ref[...] = c
    stat_tile = lax.broadcasted_iota(jnp.int32, (n_kt, N_FX), 0)
    stat_head = lax.broadcasted_iota(jnp.int32, (n_kt, N_FX), 1)
    c_end = jnp.zeros((n_kt, N_FX), F32)
    for t in range(n_kt):
        last = jnp.broadcast_to(c[(t + 1) * BK - 1:(t + 1) * BK, :], (n_kt, N_FX))
        c_end = jnp.where(stat_tile == t, last, c_end)
    cend_ref[0] = c_end

    for cidx in range(SB_K_WIDTH // 256):
        cols = slice(cidx * 256, (cidx + 1) * 256)
        k_ref[:, cols] = jnp.dot(xbf, wks_ref[:, cols],
                                 preferred_element_type=F32).astype(BF16)

    lane = lax.broadcasted_iota(jnp.int32, (bs, HEAD_PAD), 1)
    knorm2 = jnp.zeros((n_kt, N_FX), F32)
    for hp in range(N_FX // 2):
        r = jnp.dot(xbf, wkf_ref[:, hp * 256:(hp + 1) * 256],
                    preferred_element_type=F32)
        for sub in range(2):
            hh = 2 * hp + sub
            rh = r[:, sub * HEAD_PAD:(sub + 1) * HEAD_PAD]
            kr = rh.astype(BF16).astype(F32)
            n2 = jnp.sum(kr * kr, axis=1, keepdims=True)
            for t in range(n_kt):
                tmax = jnp.max(n2[t * BK:(t + 1) * BK, :], axis=0, keepdims=True)
                knorm2 = jnp.where((stat_tile == t) & (stat_head == hh),
                                   jnp.broadcast_to(tmax, (n_kt, N_FX)), knorm2)
            cb = jnp.broadcast_to(c[:, hh:hh + 1], (bs, HEAD_PAD))
            hi = cb.astype(BF16).astype(F32)
            r1 = cb - hi
            mid = r1.astype(BF16).astype(F32)
            lo = (r1 - mid).astype(BF16).astype(F32)
            aug = jnp.where(lane == HEAD_DIM, hi,
                            jnp.where(lane == HEAD_DIM + 1, mid,
                                      jnp.where(lane == HEAD_DIM + 2, lo, 0.0)))
            base = SB_K_WIDTH + hh * HEAD_PAD
            k_ref[:, base:base + HEAD_PAD] = (rh + aug).astype(BF16)

    knorm = jnp.sqrt(knorm2)
    k_pre = jnp.zeros((n_kt, N_FX), F32)
    run_max = kmax_ref[...]
    for t in range(n_kt):
        run_max = jnp.maximum(run_max, knorm[t:t + 1, :])
        k_pre = jnp.where(stat_tile == t, jnp.broadcast_to(run_max, (n_kt, N_FX)), k_pre)
    kmax_ref[...] = run_max
    kpre_ref[0] = k_pre

    pad_rows = HEAD_PAD - HEAD_DIM
    prow = lax.broadcasted_iota(jnp.int32, (pad_rows, bs), 0)
    neg_rows = jnp.where(prow < 3, -1.0, 0.0).astype(BF16)
    zero_rows = jnp.zeros((pad_rows, bs), BF16)
    for cidx in range(4):
        r = lax.dot_general(wqT_ref[cidx * 256:(cidx + 1) * 256, :], xbf, _NT,
                            preferred_element_type=F32)
        for sub in range(4):
            h = cidx * 4 + sub
            base = h * HEAD_PAD
            q_rows = (r[sub * HEAD_DIM:(sub + 1) * HEAD_DIM, :]
                      * (HEAD_DIM ** -0.5 * LOG2E)).astype(BF16)
            if h < N_SB:
                own = base + (h % 2) * HEAD_DIM
                other = base + (1 - h % 2) * HEAD_DIM
                qT_ref[own:own + HEAD_DIM, :] = q_rows
                qT_ref[other:other + HEAD_DIM, :] = zero_rows
            else:
                qT_ref[base:base + HEAD_DIM, :] = q_rows
                qT_ref[base + HEAD_DIM:base + HEAD_PAD, :] = neg_rows

    for cidx in range(4):
        r = lax.dot_general(wvT_ref[cidx * 256:(cidx + 1) * 256, :], xbf, _NT,
                            preferred_element_type=F32)
        for jb in range(n_kt):
            vT_ref[jb, cidx * 256:(cidx + 1) * 256, :] = (
                r[:, jb * BK:(jb + 1) * BK].astype(BF16))


def _projection(x2, wqT, wks, wkf, wvT, wf, bf):
    s, dm = x2.shape
    bs = BLOCK_ROWS
    n_kt = bs // BK
    return pl.pallas_call(
        _proj_kernel,
        grid=(s // bs,),
        in_specs=[
            pl.BlockSpec((bs, dm), lambda i: (i, 0)),
            _const_spec(wqT.shape),
            _const_spec(wks.shape),
            _const_spec(wkf.shape),
            _const_spec(wvT.shape),
            _const_spec(wf.shape),
            _const_spec(bf.shape),
        ],
        out_specs=[
            pl.BlockSpec((N_HEADS * HEAD_PAD, bs), lambda i: (0, i)),
            pl.BlockSpec((bs, K_WIDTH), lambda i: (i, 0)),
            pl.BlockSpec((n_kt, N_HEADS * HEAD_DIM, BK), lambda i: (i, 0, 0)),
            pl.BlockSpec((bs, N_FX), lambda i: (i, 0)),
            pl.BlockSpec((1, n_kt, N_FX), lambda i: (i, 0, 0)),
            pl.BlockSpec((1, n_kt, N_FX), lambda i: (i, 0, 0)),
        ],
        out_shape=[
            jax.ShapeDtypeStruct((N_HEADS * HEAD_PAD, s), BF16),
            jax.ShapeDtypeStruct((s, K_WIDTH), BF16),
            jax.ShapeDtypeStruct((s // BK, N_HEADS * HEAD_DIM, BK), BF16),
            jax.ShapeDtypeStruct((s, N_FX), F32),
            jax.ShapeDtypeStruct((s // bs, n_kt, N_FX), F32),
            jax.ShapeDtypeStruct((s // bs, n_kt, N_FX), F32),
        ],
        scratch_shapes=[pltpu.VMEM((1, N_FX), F32), pltpu.VMEM((1, N_FX), F32)],
        compiler_params=pltpu.CompilerParams(
            dimension_semantics=("arbitrary",), vmem_limit_bytes=VMEM_LIMIT),
        name="proj",
    )(x2, wqT, wks, wkf, wvT, wf, bf)


def _head_rmsnorm_T(oT, g_col):
    ms = jnp.mean(oT * oT, axis=0, keepdims=True)
    return oT * lax.rsqrt(ms + RMS_EPS) * g_col


def _key_before_query(n_queries, strict):
    srow = lax.broadcasted_iota(jnp.int32, (BK, n_queries), 0)
    tcol = lax.broadcasted_iota(jnp.int32, (BK, n_queries), 1)
    return srow < tcol if strict else srow <= tcol


def _key_block(k_ref, j):
    return k_ref[pl.ds(pl.multiple_of(j * BK, BK), BK), :]


def _dot(a, b):
    return jnp.dot(a, b, preferred_element_type=F32)


def _place(x, q_lo, q_hi, fill):
    pieces = []
    if q_lo:
        pieces.append(jnp.full((x.shape[0], q_lo), fill, F32))
    pieces.append(x)
    if q_hi < BQ:
        pieces.append(jnp.full((x.shape[0], BQ - q_hi), fill, F32))
    return pieces[0] if len(pieces) == 1 else jnp.concatenate(pieces, axis=1)


def _widen(x, q0, fill):
    return _place(x, q0, BQ, fill)


class _StickBreaking:
    def __init__(self, i, qT_ref, k_ref, vT_ref, tri_ref, sets, acc_ref, run_ref):
        self.qT_ref, self.k_ref, self.vT_ref, self.tri_ref = qT_ref, k_ref, vT_ref, tri_ref
        self.sets, self.acc_ref, self.run_ref = sets, acc_ref, run_ref
        self.n_tiles = NSUB * (i + 1)

    def key_tile(self, t):
        return self.n_tiles - 1 - t

    def init(self):
        self.acc_ref[...] = jnp.zeros_like(self.acc_ref)
        self.run_ref[...] = jnp.zeros_like(self.run_ref)

    def diag(self):
        return [(0, BK, BQ, _key_before_query(BQ - BK, strict=True)),
                (1, 0, BQ, _key_before_query(BQ, strict=True))]

    @staticmethod
    def softplus2(z):
        return jnp.maximum(z, 0.0) + jnp.log(1.0 + jnp.exp2(-jnp.abs(z))) * LOG2E

    @staticmethod
    def split(sp):
        hi = sp.astype(BF16)
        return hi, (sp - hi.astype(F32)).astype(BF16)

    def suffix_sums(self, hi, lo):
        tri = self.tri_ref[...]
        return _dot(tri, hi) + _dot(tri, lo)

    def scores(self, descs):
        return [_dot(_key_block(self.k_ref, self.key_tile(t)), self.qT_ref[:, q_lo:q_hi])
                for t, q_lo, q_hi, _ in descs]

    def splits(self, descs, zs):
        parts = []
        for desc, z in zip(descs, zs):
            sp = self.softplus2(z)
            if desc[3] is not None:
                sp = jnp.where(desc[3], sp, 0.0)
            parts.append(self.split(sp))
        return parts

    def finish(self, descs, zs, csums):
        ws = []
        for desc, z, csum in zip(descs, zs, csums):
            w = jnp.exp2(z - csum)
            if desc[3] is not None:
                w = jnp.where(desc[3], w, 0.0)
            ws.append(w.astype(BF16))
        pvs = [_dot(self.vT_ref[self.key_tile(desc[0])], w) for desc, w in zip(descs, ws)]
        run = self.run_ref[...]
        acc = self.acc_ref[...]
        for (_, q_lo, q_hi, _), csum, pv in zip(descs, csums, pvs):
            acc = acc + _place(pv, q_lo, q_hi, 0.0) * jnp.exp2(-run)
            run = run + _place(csum[0:1, :], q_lo, q_hi, 0.0)
        self.acc_ref[...] = acc
        self.run_ref[...] = run

    def batch(self, descs):
        zs = self.scores(descs)
        parts = self.splits(descs, zs)
        self.finish(descs, zs, [self.suffix_sums(hi, lo) for hi, lo in parts])

    def step(self, score_t=None, csum_t=None, pv_t=None):
        if score_t is not None:
            z = _dot(_key_block(self.k_ref, self.key_tile(score_t[0])), self.qT_ref[...])
        if csum_t is not None:
            z_ref, hi_ref, lo_ref, w_ref, r_ref = self.sets[csum_t[1]]
            csum = self.suffix_sums(hi_ref[...], lo_ref[...])
        if pv_t is not None:
            pv_set = self.sets[pv_t[1]]
            self.acc_ref[...] += pv_set[4][...] * _dot(
                self.vT_ref[self.key_tile(pv_t[0])], pv_set[3][...])
        if score_t is not None:
            zs_ref, his_ref, los_ref, _, _ = self.sets[score_t[1]]
            hi, lo = self.split(self.softplus2(z))
            zs_ref[...] = z
            his_ref[...] = hi
            los_ref[...] = lo
        if csum_t is not None:
            w_ref[...] = jnp.exp2(z_ref[...] - csum).astype(BF16)
            run = self.run_ref[...]
            r_ref[...] = jnp.exp2(-run)
            self.run_ref[...] = run + csum[0:1, :]

    def alive_in(self, q_lo, q_hi):
        return jnp.min(self.run_ref[:, q_lo:q_hi]) < SB_DEAD_LOG2

    def alive(self):
        return self.alive_in(0, BQ)

    def rest(self, enter):
        @pl.when(enter)
        def _():
            self._rest_if_alive()

    def _rest_if_alive(self):
        @pl.when(self.alive())
        def _():
            self.batch([(3, 0, BQ, None)])

            @pl.when(jnp.logical_and(self.alive(), self.n_tiles > 4))
            def _():
                self.step(score_t=(4, 0))
                self.step(score_t=(5, 1), csum_t=(4, 0))

                def cond(carry):
                    return jnp.logical_and(carry[0] < self.n_tiles, carry[1] > 0)

                def body(carry):
                    t = carry[0]
                    still_alive = self.alive().astype(jnp.int32)
                    self.step(score_t=(t, 0), csum_t=(t - 1, 1), pv_t=(t - 2, 0))
                    self.step(score_t=(t + 1, 1), csum_t=(t, 0), pv_t=(t - 1, 1))
                    return t + 2, still_alive

                t_end, _ = lax.while_loop(cond, body, (jnp.int32(6), jnp.int32(1)))
                self.step(csum_t=(t_end - 1, 1), pv_t=(t_end - 2, 0))
                self.step(pv_t=(t_end - 1, 1))


class _Forgetting:
    def __init__(self, h, i, kpre_ref, cend_ref, qT_ref, k_ref, vT_ref, ct_ref, sets,
                 acc_ref, m_ref, l_ref, qn_ref):
        self.h, self.kpre_ref, self.cend_ref = h, kpre_ref, cend_ref
        self.qT_ref, self.k_ref, self.vT_ref, self.ct_ref = qT_ref, k_ref, vT_ref, ct_ref
        self.sets, self.acc_ref, self.m_ref, self.l_ref, self.qn_ref = (
            sets, acc_ref, m_ref, l_ref, qn_ref)
        self.n_tiles = NSUB * (i + 1)
        self.d0 = NSUB * i

    def key_tile(self, t):
        return self.n_tiles - 1 - t

    def init(self):
        self.acc_ref[...] = jnp.zeros_like(self.acc_ref)
        self.l_ref[...] = jnp.zeros_like(self.l_ref)
        self.m_ref[...] = jnp.full(self.m_ref.shape, -jnp.inf, F32)

    def set_query_norms(self):
        qf = self.qT_ref[0:HEAD_DIM, :].astype(F32)
        self.qn_ref[...] = jnp.sqrt(jnp.sum(qf * qf, axis=0, keepdims=True))

    def diag(self):
        return [(self.d0, 0, _key_before_query(BQ, strict=False)),
                (self.d0 + 1, BK, _key_before_query(BQ - BK, strict=False))]

    def scores(self, descs):
        return [_dot(_key_block(self.k_ref, j), self.qT_ref[:, q0:]) for j, q0, _ in descs]

    def soft_group(self, descs, us):
        us = [u if mask is None else jnp.where(mask, u, -jnp.inf)
              for (_, _, mask), u in zip(descs, us)]
        mx = None
        for (_, q0, _), u in zip(descs, us):
            cm = _widen(jnp.max(u, axis=0, keepdims=True), q0, -jnp.inf)
            mx = cm if mx is None else jnp.maximum(mx, cm)
        ct = self.ct_ref[...]
        m_old = self.m_ref[...]
        m_new = jnp.maximum(m_old, mx + ct)
        off = ct - m_new
        alpha = jnp.exp2(m_old - m_new)
        l = alpha * self.l_ref[...]
        acc = alpha * self.acc_ref[...]
        for (j, q0, _), u in zip(descs, us):
            p = jnp.exp2(u + off[:, q0:])
            l = l + _widen(jnp.sum(p, axis=0, keepdims=True), q0, 0.0)
            acc = acc + _widen(_dot(self.vT_ref[j], p.astype(BF16)), q0, 0.0)
        self.m_ref[...] = m_new
        self.l_ref[...] = l
        self.acc_ref[...] = acc

    def batch(self, groups):
        scores = [self.scores(descs) for descs in groups]
        for descs, us in zip(groups, scores):
            self.soft_group(descs, us)

    def step(self, score_t=None, soft_t=None, pv_t=None):
        if score_t is not None:
            u = _dot(_key_block(self.k_ref, score_t[0]), self.qT_ref[...])
        if pv_t is not None:
            pv_set = self.sets[pv_t[1]]
            self.acc_ref[...] = pv_set[3][...] * self.acc_ref[...] + _dot(
                self.vT_ref[pv_t[0]], pv_set[2][...])
        if score_t is not None:
            us_ref, mxs_ref, _, _ = self.sets[score_t[1]]
            us_ref[...] = u
            mxs_ref[...] = jnp.max(u, axis=0, keepdims=True)
        if soft_t is not None:
            u_ref, mx_ref, p_ref, al_ref = self.sets[soft_t[1]]
            ct = self.ct_ref[...]
            m_old = self.m_ref[...]
            m_new = jnp.maximum(m_old, mx_ref[...] + ct)
            p = jnp.exp2(u_ref[...] + (ct - m_new))
            alpha = jnp.exp2(m_old - m_new)
            self.m_ref[...] = m_new
            self.l_ref[...] = alpha * self.l_ref[...] + jnp.sum(p, axis=0, keepdims=True)
            p_ref[...] = p.astype(BF16)
            al_ref[...] = alpha

    def alive(self, t):
        j = jnp.maximum(self.key_tile(t), 0)
        kmax = self.kpre_ref[self.h, j]
        c_end = self.cend_ref[self.h, j]
        bound = (self.qn_ref[...] * (kmax * FX_NORM_SLACK)
                 + (self.ct_ref[...] - c_end) - self.m_ref[...])
        return jnp.max(bound) >= -(FX_DEAD_LOG2 + 1e-5 * jnp.abs(c_end))

    def more_after_four(self):
        return jnp.logical_and(self.alive(4), self.n_tiles > 4)

    def rest(self, enter):
        key_tile = self.key_tile

        @pl.when(enter)
        def _():
            self.step(score_t=(key_tile(4), 0))
            self.step(score_t=(key_tile(5), 1), soft_t=(key_tile(4), 0))

            def cond(carry):
                return jnp.logical_and(carry[0] < self.n_tiles, carry[1] > 0)

            def body(carry):
                t = carry[0]
                next_alive = self.alive(t + 2).astype(jnp.int32)
                self.step(score_t=(key_tile(t), 0), soft_t=(key_tile(t - 1), 1),
                          pv_t=(key_tile(t - 2), 0))
                self.step(score_t=(key_tile(t + 1), 1), soft_t=(key_tile(t), 0),
                          pv_t=(key_tile(t - 1), 1))
                return t + 2, next_alive

            t_end, _ = lax.while_loop(
                cond, body, (jnp.int32(6), self.alive(6).astype(jnp.int32)))
            self.step(soft_t=(key_tile(t_end - 1), 1), pv_t=(key_tile(t_end - 2), 0))
            self.step(pv_t=(key_tile(t_end - 1), 1))


Q_BLOCKS_PER_STEP = 2


def _fast_tiles(sb, fx):
    fx.set_query_norms()
    s_descs = sb.diag() + [(2, 0, BK, None)]
    f_groups = [fx.diag(), [(fx.key_tile(2), 0, None), (fx.key_tile(3), 0, None)]]
    s_z = sb.scores(s_descs)
    f_u = [fx.scores(descs) for descs in f_groups]
    parts = sb.splits(s_descs, s_z)
    csums = [sb.suffix_sums(*parts[0])]
    fx.soft_group(f_groups[0], f_u[0])
    csums += [sb.suffix_sums(hi, lo) for hi, lo in parts[1:]]
    fx.soft_group(f_groups[1], f_u[1])
    sb.finish(s_descs, s_z, csums)


def _remaining_tiles(sb, fx):
    late_alive = sb.alive_in(BK, BQ)
    early_alive = sb.alive_in(0, BK)
    fx_more = fx.more_after_four()

    @pl.when(late_alive)
    def _():
        sb.batch([(2, BK, BQ, None)])

    sb.rest(jnp.logical_or(late_alive, early_alive))
    fx.rest(fx_more)


def _attn_kernel(kpre_ref, cend_ref, qs_ref, qf_ref, ks_ref, kf_ref, vs_ref, vf_ref,
                 tri_ref, ct_ref, gs_ref, gf_ref, os_ref, of_ref, *scratch):
    h = pl.program_id(0)
    step = pl.program_id(1)
    sb_sets = (scratch[0:5], scratch[5:10])
    fx_sets = (scratch[10:14], scratch[14:18])
    state = scratch[18:]
    blocks = []
    for slot in range(Q_BLOCKS_PER_STEP):
        i = step * Q_BLOCKS_PER_STEP + slot
        lanes = pl.ds(slot * BQ, BQ)
        sb_acc, sb_run, fx_acc, fx_m, fx_l, fx_qn = state[6 * slot:6 * slot + 6]
        sb = _StickBreaking(i, qs_ref.at[:, lanes], ks_ref, vs_ref, tri_ref, sb_sets,
                            sb_acc, sb_run)
        fx = _Forgetting(h, i, kpre_ref, cend_ref, qf_ref.at[:, lanes], kf_ref, vf_ref,
                         ct_ref.at[:, lanes], fx_sets, fx_acc, fx_m, fx_l, fx_qn)
        sb.init()
        fx.init()
        blocks.append((sb, fx))

    @pl.when(step == 0)
    def _():
        sb0, fx0 = blocks[0]
        sb0.batch(sb0.diag())
        fx0.batch([fx0.diag()])
        for sb, fx in blocks[1:]:
            _fast_tiles(sb, fx)
            _remaining_tiles(sb, fx)

    @pl.when(step > 0)
    def _():
        for sb, fx in blocks:
            _fast_tiles(sb, fx)
            _remaining_tiles(sb, fx)

    for slot, (sb, fx) in enumerate(blocks):
        lanes = slice(slot * BQ, (slot + 1) * BQ)
        os_ref[:, lanes] = _head_rmsnorm_T(sb.acc_ref[...], gs_ref[...]).astype(os_ref.dtype)
        of_ref[:, lanes] = _head_rmsnorm_T(fx.acc_ref[...] / fx.l_ref[...],
                                           gf_ref[...]).astype(of_ref.dtype)


def _attn_scratch():
    tile_f32 = pltpu.VMEM((BK, BQ), F32)
    tile_bf16 = pltpu.VMEM((BK, BQ), BF16)
    row = pltpu.VMEM((1, BQ), F32)
    acc = pltpu.VMEM((HEAD_DIM, BQ), F32)
    sb_set = [tile_f32, tile_bf16, tile_bf16, tile_bf16, row]
    fx_set = [tile_f32, row, tile_bf16, row]
    per_block = [acc, row, acc, row, row, row]
    return sb_set + sb_set + fx_set + fx_set + per_block * Q_BLOCKS_PER_STEP


def _attention(qT, k, vT, tri, ct, kpre, cend, g_sb_col, g_fx_col):
    s = k.shape[0]
    bq = BQ * Q_BLOCKS_PER_STEP
    nkb = s // BK
    smem_spec = pl.BlockSpec(memory_space=pltpu.SMEM)
    out_spec = pl.BlockSpec((HEAD_DIM, bq), lambda h, i: (h, i))
    out_shape = jax.ShapeDtypeStruct((N_SB * HEAD_DIM, s), BF16)
    return pl.pallas_call(
        _attn_kernel,
        grid=(N_SB, s // bq),
        in_specs=[
            smem_spec, smem_spec,
            pl.BlockSpec((HEAD_PAD, bq), lambda h, i: (h, i)),
            pl.BlockSpec((HEAD_PAD, bq), lambda h, i: (N_SB + h, i)),
            pl.BlockSpec((s, HEAD_PAD), lambda h, i: (0, h // 2)),
            pl.BlockSpec((s, HEAD_PAD), lambda h, i: (0, SB_K_WIDTH // HEAD_PAD + h)),
            pl.BlockSpec((nkb, HEAD_DIM, BK), lambda h, i: (0, h, 0)),
            pl.BlockSpec((nkb, HEAD_DIM, BK), lambda h, i: (0, N_SB + h, 0)),
            pl.BlockSpec((BK, BK), lambda h, i: (0, 0)),
            pl.BlockSpec((None, 1, bq), lambda h, i: (h, 0, i)),
            pl.BlockSpec((HEAD_DIM, 1), lambda h, i: (h, 0)),
            pl.BlockSpec((HEAD_DIM, 1), lambda h, i: (h, 0)),
        ],
        out_specs=[out_spec, out_spec],
        out_shape=[out_shape, out_shape],
        scratch_shapes=_attn_scratch(),
        compiler_params=pltpu.CompilerParams(
            dimension_semantics=("arbitrary", "arbitrary"),
            vmem_limit_bytes=VMEM_LIMIT),
        name="attn",
    )(kpre, cend, qT, qT, k, k, vT, vT, tri, ct, g_sb_col, g_fx_col)


def _layer_norm(v, g, b):
    mu = jnp.mean(v, axis=-1, keepdims=True)
    vc = v - mu
    var = jnp.mean(vc * vc, axis=-1, keepdims=True)
    return vc * lax.rsqrt(var + LN_EPS) * g + b


def _ffn_kernel(alpha, x_ref, osb_ref, ofx_ref, woa_ref, wob_ref,
                g1_ref, b1_ref, g2_ref, b2_ref, wgu_ref, wd_ref,
                y_ref, act_ref):
    d_ff = wd_ref.shape[0]
    mix = (lax.dot_general(osb_ref[...], woa_ref[...], _TN, preferred_element_type=F32)
           + lax.dot_general(ofx_ref[...], wob_ref[...], _TN, preferred_element_type=F32))
    h1 = _layer_norm(alpha * x_ref[...] + mix, g1_ref[...], b1_ref[...])
    h1b = h1.astype(BF16)
    for c in range(d_ff // FF_CHUNK):
        lo, hi = c * FF_CHUNK, (c + 1) * FF_CHUNK
        gate = jnp.dot(h1b, wgu_ref[:, lo:hi], preferred_element_type=F32)
        up = jnp.dot(h1b, wgu_ref[:, d_ff + lo:d_ff + hi], preferred_element_type=F32)
        act_ref[:, lo:hi] = (gate * jax.nn.sigmoid(gate) * up).astype(BF16)
    ff = jnp.dot(act_ref[...], wd_ref[...], preferred_element_type=F32)
    y_ref[...] = _layer_norm(alpha * h1 + ff, g2_ref[...], b2_ref[...])


def _out_ffn(alpha, x2, oT_sb, oT_fx, wo_a, wo_b, g1, b1, g2, b2, wgu, wd):
    s, dm = x2.shape
    bs = BLOCK_ROWS
    d_ff = wd.shape[0]
    half = oT_sb.shape[0]
    return pl.pallas_call(
        functools.partial(_ffn_kernel, alpha),
        grid=(s // bs,),
        in_specs=[
            pl.BlockSpec((bs, dm), lambda i: (i, 0)),
            pl.BlockSpec((half, bs), lambda i: (0, i)),
            pl.BlockSpec((half, bs), lambda i: (0, i)),
            _const_spec(wo_a.shape),
            _const_spec(wo_b.shape),
            _const_spec(g1.shape),
            _const_spec(b1.shape),
            _const_spec(g2.shape),
            _const_spec(b2.shape),
            _const_spec(wgu.shape),
            _const_spec(wd.shape),
        ],
        out_specs=pl.BlockSpec((bs, dm), lambda i: (i, 0)),
        out_shape=jax.ShapeDtypeStruct((s, dm), F32),
        scratch_shapes=[pltpu.VMEM((bs, d_ff), BF16)],
        compiler_params=pltpu.CompilerParams(
            dimension_semantics=("arbitrary",), vmem_limit_bytes=VMEM_LIMIT),
        name="out_ffn",
    )(x2, oT_sb, oT_fx, wo_a, wo_b, g1, b1, g2, b2, wgu, wd)


def _layer(x2, w_in, b_f, g_sb, g_fox, w_out, ln1_g, ln1_b, ln2_g, ln2_b,
           w_gate_up, w_down, alpha):
    assert NSUB == 2
    s, dm = x2.shape
    sbw = N_SB * HEAD_DIM
    fxw = N_FX * HEAD_DIM
    q_sb, k_sb, v_sb = w_in[:, :sbw], w_in[:, sbw:2 * sbw], w_in[:, 2 * sbw:3 * sbw]
    o = 3 * sbw
    q_fx, k_fx, v_fx = w_in[:, o:o + fxw], w_in[:, o + fxw:o + 2 * fxw], w_in[:, o + 2 * fxw:o + 3 * fxw]
    wf = w_in[:, o + 3 * fxw:]

    wqT = jnp.concatenate([q_sb, q_fx], axis=1).T.astype(BF16)
    wks = k_sb.astype(BF16)
    wkf = jnp.pad(k_fx.reshape(dm, N_FX, HEAD_DIM), ((0, 0), (0, 0), (0, HEAD_PAD - HEAD_DIM)))
    wkf = wkf.reshape(dm, N_FX * HEAD_PAD).astype(BF16)
    wvT = jnp.concatenate([v_sb, v_fx], axis=1).T.astype(BF16)

    wf_hi = wf.astype(BF16)
    wf_lo = (wf - wf_hi.astype(F32)).astype(BF16)
    wf2 = jnp.concatenate([wf_hi, wf_lo], axis=1)

    qT, k, vT, c, kpre, cend = _projection(x2, wqT, wks, wkf, wvT, wf2,
                                           b_f.reshape(1, N_FX))

    srow = lax.broadcasted_iota(jnp.int32, (BK, BK), 0)
    jcol = lax.broadcasted_iota(jnp.int32, (BK, BK), 1)
    tri = (jcol >= srow).astype(BF16)
    ct = c.T.reshape(N_FX, 1, s)
    kpre = kpre.reshape(s // BK, N_FX).T
    cend = cend.reshape(s // BK, N_FX).T

    oT_sb, oT_fx = _attention(qT, k, vT, tri, ct, kpre, cend,
                              g_sb.reshape(sbw, 1), g_fox.reshape(fxw, 1))

    wo = w_out.astype(BF16)
    return _out_ffn(
        alpha, x2, oT_sb, oT_fx, wo[:sbw], wo[sbw:],
        ln1_g.reshape(1, dm), ln1_b.reshape(1, dm),
        ln2_g.reshape(1, dm), ln2_b.reshape(1, dm),
        w_gate_up.astype(BF16), w_down.astype(BF16))


def kernel(x, w_in, b_f, g_sb, g_fox, w_out, ln1_g, ln1_b, ln2_g, ln2_b, w_gate_up, w_down):
    batch, s, dm = x.shape
    depth = w_in.shape[0]
    alpha = (2 * depth) ** 0.25
    outs = []
    for b in range(batch):
        h = x[b]
        for l in range(depth):
            h = _layer(h, w_in[l], b_f[l], g_sb[l], g_fox[l], w_out[l],
                       ln1_g[l], ln1_b[l], ln2_g[l], ln2_b[l],
                       w_gate_up[l], w_down[l], alpha)
        outs.append(h)
    return outs[0][None] if batch == 1 else jnp.stack(outs, axis=0)
```

```python
import functools

import jax
import jax.numpy as jnp
from jax import lax
from jax.experimental import pallas as pl
from jax.experimental.pallas import tpu as pltpu

F32 = jnp.float32
BF16 = jnp.bfloat16

HEAD_DIM = 64
HEAD_PAD = 128
N_SB = 8
N_FX = 8
N_HEADS = N_SB + N_FX
SB_K_WIDTH = N_SB * HEAD_DIM
K_WIDTH = SB_K_WIDTH + N_FX * HEAD_PAD
LN_EPS = 1e-5
RMS_EPS = 1e-6
LOG2E = 1.4426950408889634

BLOCK_ROWS = 512
BQ = 512
BK = 256
NSUB = BQ // BK
FF_CHUNK = 256
VMEM_LIMIT = 56 * 1024 * 1024

SB_DEAD_LOG2 = 160.0
FX_DEAD_LOG2 = 160.0
ONES_ROWS = 16
FX_NORM_SLACK = 1.001

_NT = (((1,), (1,)), ((), ()))
_TN = (((0,), (0,)), ((), ()))


def _const_spec(shape):
    nd = len(shape)
    return pl.BlockSpec(shape, lambda *_: (0,) * nd, pipeline_mode=pl.Buffered(1))


def _proj_kernel(x_ref, wqT_ref, wks_ref, wkf_ref, wvT_ref, wf_ref, bf_ref,
                 qT_ref, k_ref, vT_ref, c_ref, kpre_ref, cend_ref,
                 carry_ref, kmax_ref):
    i = pl.program_id(0)
    bs = x_ref.shape[0]
    n_kt = bs // BK

    @pl.when(i == 0)
    def _():
        carry_ref[...] = jnp.zeros_like(carry_ref)
        kmax_ref[...] = jnp.zeros_like(kmax_ref)

    xb = x_ref[...]
    xbf = xb.astype(BF16)

    x_lo = (xb - xbf.astype(F32)).astype(BF16)
    wf2 = wf_ref[...]
    f_hi = jnp.dot(xbf, wf2, preferred_element_type=F32)
    f_lo = jnp.dot(x_lo, wf2[:, :N_FX], preferred_element_type=F32)
    f = f_hi[:, :N_FX] + f_hi[:, N_FX:] + f_lo + bf_ref[...]
    logf = jnp.minimum(f, 0.0) - jnp.log1p(jnp.exp(-jnp.abs(f)))
    row = lax.broadcasted_iota(jnp.int32, logf.shape, 0)
    c = logf
    d = 1
    while d < bs:
        c = c + jnp.where(row >= d, pltpu.roll(c, d, axis=0), 0.0)
        d *= 2
    c = c + carry_ref[...]
    carry_ref[...] = c[bs - 1:bs, :]
    c = c * LOG2E
    c_ref[...] = c
    stat_tile = lax.broadcasted_iota(jnp.int32, (n_kt, N_FX), 0)
    stat_head = lax.broadcasted_iota(jnp.int32, (n_kt, N_FX), 1)
    c_end = jnp.zeros((n_kt, N_FX), F32)
    for t in range(n_kt):
        last = jnp.broadcast_to(c[(t + 1) * BK - 1:(t + 1) * BK, :], (n_kt, N_FX))
        c_end = jnp.where(stat_tile == t, last, c_end)
    cend_ref[0] = c_end

    for cidx in range(SB_K_WIDTH // 256):
        cols = slice(cidx * 256, (cidx + 1) * 256)
        k_ref[:, cols] = jnp.dot(xbf, wks_ref[:, cols],
                                 preferred_element_type=F32).astype(BF16)

    lane = lax.broadcasted_iota(jnp.int32, (bs, HEAD_PAD), 1)
    knorm2 = jnp.zeros((n_kt, N_FX), F32)
    for hp in range(N_FX // 2):
        r = jnp.dot(xbf, wkf_ref[:, hp * 256:(hp + 1) * 256],
                    preferred_element_type=F32)
        for sub in range(2):
            hh = 2 * hp + sub
            rh = r[:, sub * HEAD_PAD:(sub + 1) * HEAD_PAD]
            kr = rh.astype(BF16).astype(F32)
            n2 = jnp.sum(kr * kr, axis=1, keepdims=True)
            for t in range(n_kt):
                tmax = jnp.max(n2[t * BK:(t + 1) * BK, :], axis=0, keepdims=True)
                knorm2 = jnp.where((stat_tile == t) & (stat_head == hh),
                                   jnp.broadcast_to(tmax, (n_kt, N_FX)), knorm2)
            cb = jnp.broadcast_to(c[:, hh:hh + 1], (bs, HEAD_PAD))
            hi = cb.astype(BF16).astype(F32)
            r1 = cb - hi
            mid = r1.astype(BF16).astype(F32)
            lo = (r1 - mid).astype(BF16).astype(F32)
            aug = jnp.where(lane == HEAD_DIM, hi,
                            jnp.where(lane == HEAD_DIM + 1, mid,
                                      jnp.where(lane == HEAD_DIM + 2, lo, 0.0)))
            base = SB_K_WIDTH + hh * HEAD_PAD
            k_ref[:, base:base + HEAD_PAD] = (rh + aug).astype(BF16)

    knorm = jnp.sqrt(knorm2)
    k_pre = jnp.zeros((n_kt, N_FX), F32)
    run_max = kmax_ref[...]
    for t in range(n_kt):
        run_max = jnp.maximum(run_max, knorm[t:t + 1, :])
        k_pre = jnp.where(stat_tile == t, jnp.broadcast_to(run_max, (n_kt, N_FX)), k_pre)
    kmax_ref[...] = run_max
    kpre_ref[0] = k_pre

    pad_rows = HEAD_PAD - HEAD_DIM
    prow = lax.broadcasted_iota(jnp.int32, (pad_rows, bs), 0)
    neg_rows = jnp.where(prow < 3, -1.0, 0.0).astype(BF16)
    zero_rows = jnp.zeros((pad_rows, bs), BF16)
    for cidx in range(4):
        r = lax.dot_general(wqT_ref[cidx * 256:(cidx + 1) * 256, :], xbf, _NT,
                            preferred_element_type=F32)
        for sub in range(4):
            h = cidx * 4 + sub
            base = h * HEAD_PAD
            q_rows = (r[sub * HEAD_DIM:(sub + 1) * HEAD_DIM, :]
                      * (HEAD_DIM ** -0.5 * LOG2E)).astype(BF16)
            if h < N_SB:
                own = base + (h % 2) * HEAD_DIM
                other = base + (1 - h % 2) * HEAD_DIM
                qT_ref[own:own + HEAD_DIM, :] = q_rows
                qT_ref[other:other + HEAD_DIM, :] = zero_rows
            else:
                qT_ref[base:base + HEAD_DIM, :] = q_rows
                qT_ref[base + HEAD_DIM:base + HEAD_PAD, :] = neg_rows

    for cidx in range(4):
        r = lax.dot_general(wvT_ref[cidx * 256:(cidx + 1) * 256, :], xbf, _NT,
                            preferred_element_type=F32)
        for jb in range(n_kt):
            vT_ref[jb, cidx * 256:(cidx + 1) * 256, :] = (
                r[:, jb * BK:(jb + 1) * BK].astype(BF16))


def _projection(x2, wqT, wks, wkf, wvT, wf, bf):
    s, dm = x2.shape
    bs = BLOCK_ROWS
    n_kt = bs // BK
    return pl.pallas_call(
        _proj_kernel,
        grid=(s // bs,),
        in_specs=[
            pl.BlockSpec((bs, dm), lambda i: (i, 0)),
            _const_spec(wqT.shape),
            _const_spec(wks.shape),
            _const_spec(wkf.shape),
            _const_spec(wvT.shape),
            _const_spec(wf.shape),
            _const_spec(bf.shape),
        ],
        out_specs=[
            pl.BlockSpec((N_HEADS * HEAD_PAD, bs), lambda i: (0, i)),
            pl.BlockSpec((bs, K_WIDTH), lambda i: (i, 0)),
            pl.BlockSpec((n_kt, N_HEADS * HEAD_DIM, BK), lambda i: (i, 0, 0)),
            pl.BlockSpec((bs, N_FX), lambda i: (i, 0)),
            pl.BlockSpec((1, n_kt, N_FX), lambda i: (i, 0, 0)),
            pl.BlockSpec((1, n_kt, N_FX), lambda i: (i, 0, 0)),
        ],
        out_shape=[
            jax.ShapeDtypeStruct((N_HEADS * HEAD_PAD, s), BF16),
            jax.ShapeDtypeStruct((s, K_WIDTH), BF16),
            jax.ShapeDtypeStruct((s // BK, N_HEADS * HEAD_DIM, BK), BF16),
            jax.ShapeDtypeStruct((s, N_FX), F32),
            jax.ShapeDtypeStruct((s // bs, n_kt, N_FX), F32),
            jax.ShapeDtypeStruct((s // bs, n_kt, N_FX), F32),
        ],
        scratch_shapes=[pltpu.VMEM((1, N_FX), F32), pltpu.VMEM((1, N_FX), F32)],
        compiler_params=pltpu.CompilerParams(
            dimension_semantics=("arbitrary",), vmem_limit_bytes=VMEM_LIMIT),
        name="proj",
    )(x2, wqT, wks, wkf, wvT, wf, bf)


def _head_rmsnorm_T(oT, g_col):
    ms = jnp.mean(oT * oT, axis=0, keepdims=True)
    return oT * lax.rsqrt(ms + RMS_EPS) * g_col


def _key_before_query(n_queries, strict):
    srow = lax.broadcasted_iota(jnp.int32, (BK, n_queries), 0)
    tcol = lax.broadcasted_iota(jnp.int32, (BK, n_queries), 1)
    return srow < tcol if strict else srow <= tcol


def _key_block(k_ref, j):
    return k_ref[pl.ds(pl.multiple_of(j * BK, BK), BK), :]


def _dot(a, b):
    return jnp.dot(a, b, preferred_element_type=F32)


def _place(x, q_lo, q_hi, fill):
    pieces = []
    if q_lo:
        pieces.append(jnp.full((x.shape[0], q_lo), fill, F32))
    pieces.append(x)
    if q_hi < BQ:
        pieces.append(jnp.full((x.shape[0], BQ - q_hi), fill, F32))
    return pieces[0] if len(pieces) == 1 else jnp.concatenate(pieces, axis=1)


def _widen(x, q0, fill):
    return _place(x, q0, BQ, fill)


class _StickBreaking:
    def __init__(self, i, qT_ref, k_ref, vT_ref, tri_ref, sets, acc_ref, run_ref):
        self.qT_ref, self.k_ref, self.vT_ref, self.tri_ref = qT_ref, k_ref, vT_ref, tri_ref
        self.sets, self.acc_ref, self.run_ref = sets, acc_ref, run_ref
        self.n_tiles = NSUB * (i + 1)

    def key_tile(self, t):
        return self.n_tiles - 1 - t

    def init(self):
        self.acc_ref[...] = jnp.zeros_like(self.acc_ref)
        self.run_ref[...] = jnp.zeros_like(self.run_ref)

    def diag(self):
        return [(0, BK, BQ, _key_before_query(BQ - BK, strict=True)),
                (1, 0, BQ, _key_before_query(BQ, strict=True))]

    @staticmethod
    def softplus2(z):
        return jnp.maximum(z, 0.0) + jnp.log(1.0 + jnp.exp2(-jnp.abs(z))) * LOG2E

    @staticmethod
    def split(sp):
        hi = sp.astype(BF16)
        return hi, (sp - hi.astype(F32)).astype(BF16)

    def suffix_sums(self, hi, lo):
        tri = self.tri_ref[...]
        return _dot(tri, hi) + _dot(tri, lo)

    def scores(self, descs):
        return [_dot(_key_block(self.k_ref, self.key_tile(t)), self.qT_ref[:, q_lo:q_hi])
                for t, q_lo, q_hi, _ in descs]

    def splits(self, descs, zs):
        parts = []
        for desc, z in zip(descs, zs):
            sp = self.softplus2(z)
            if desc[3] is not None:
                sp = jnp.where(desc[3], sp, 0.0)
            parts.append(self.split(sp))
        return parts

    def finish(self, descs, zs, csums):
        ws = []
        for desc, z, csum in zip(descs, zs, csums):
            w = jnp.exp2(z - csum)
            if desc[3] is not None:
                w = jnp.where(desc[3], w, 0.0)
            ws.append(w.astype(BF16))
        pvs = [_dot(self.vT_ref[self.key_tile(desc[0])], w) for desc, w in zip(descs, ws)]
        run = self.run_ref[...]
        acc = self.acc_ref[...]
        for (_, q_lo, q_hi, _), csum, pv in zip(descs, csums, pvs):
            acc = acc + _place(pv, q_lo, q_hi, 0.0) * jnp.exp2(-run)
            run = run + _place(csum[0:1, :], q_lo, q_hi, 0.0)
        self.acc_ref[...] = acc
        self.run_ref[...] = run

    def batch(self, descs):
        zs = self.scores(descs)
        parts = self.splits(descs, zs)
        self.finish(descs, zs, [self.suffix_sums(hi, lo) for hi, lo in parts])

    def step(self, score_t=None, csum_t=None, pv_t=None):
        if score_t is not None:
            z = _dot(_key_block(self.k_ref, self.key_tile(score_t[0])), self.qT_ref[...])
        if csum_t is not None:
            z_ref, hi_ref, lo_ref, w_ref, r_ref = self.sets[csum_t[1]]
            csum = self.suffix_sums(hi_ref[...], lo_ref[...])
        if pv_t is not None:
            pv_set = self.sets[pv_t[1]]
            self.acc_ref[...] += pv_set[4][...] * _dot(
                self.vT_ref[self.key_tile(pv_t[0])], pv_set[3][...])
        if score_t is not None:
            zs_ref, his_ref, los_ref, _, _ = self.sets[score_t[1]]
            hi, lo = self.split(self.softplus2(z))
            zs_ref[...] = z
            his_ref[...] = hi
            los_ref[...] = lo
        if csum_t is not None:
            w_ref[...] = jnp.exp2(z_ref[...] - csum).astype(BF16)
            run = self.run_ref[...]
            r_ref[...] = jnp.exp2(-run)
            self.run_ref[...] = run + csum[0:1, :]

    def alive_in(self, q_lo, q_hi):
        return jnp.min(self.run_ref[:, q_lo:q_hi]) < SB_DEAD_LOG2

    def alive(self):
        return self.alive_in(0, BQ)

    def rest(self, enter):
        @pl.when(enter)
        def _():
            self._rest_if_alive()

    def _rest_if_alive(self):
        @pl.when(self.alive())
        def _():
            self.batch([(3, 0, BQ, None)])

            @pl.when(jnp.logical_and(self.alive(), self.n_tiles > 4))
            def _():
                self.step(score_t=(4, 0))
                self.step(score_t=(5, 1), csum_t=(4, 0))

                def cond(carry):
                    return jnp.logical_and(carry[0] < self.n_tiles, carry[1] > 0)

                def body(carry):
                    t = carry[0]
                    still_alive = self.alive().astype(jnp.int32)
                    self.step(score_t=(t, 0), csum_t=(t - 1, 1), pv_t=(t - 2, 0))
                    self.step(score_t=(t + 1, 1), csum_t=(t, 0), pv_t=(t - 1, 1))
                    return t + 2, still_alive

                t_end, _ = lax.while_loop(cond, body, (jnp.int32(6), jnp.int32(1)))
                self.step(csum_t=(t_end - 1, 1), pv_t=(t_end - 2, 0))
                self.step(pv_t=(t_end - 1, 1))


class _Forgetting:
    def __init__(self, h, i, kpre_ref, cend_ref, qT_ref, k_ref, vT_ref, ct_ref, sets,
                 acc_ref, m_ref, l_ref, qn_ref):
        self.h, self.kpre_ref, self.cend_ref = h, kpre_ref, cend_ref
        self.qT_ref, self.k_ref, self.vT_ref, self.ct_ref = qT_ref, k_ref, vT_ref, ct_ref
        self.sets, self.acc_ref, self.m_ref, self.l_ref, self.qn_ref = (
            sets, acc_ref, m_ref, l_ref, qn_ref)
        self.n_tiles = NSUB * (i + 1)
        self.d0 = NSUB * i

    def key_tile(self, t):
        return self.n_tiles - 1 - t

    def init(self):
        self.acc_ref[...] = jnp.zeros_like(self.acc_ref)
        self.l_ref[...] = jnp.zeros_like(self.l_ref)
        self.m_ref[...] = jnp.full(self.m_ref.shape, -jnp.inf, F32)

    def set_query_norms(self):
        qf = self.qT_ref[0:HEAD_DIM, :].astype(F32)
        self.qn_ref[...] = jnp.sqrt(jnp.sum(qf * qf, axis=0, keepdims=True))

    def diag(self):
        return [(self.d0, 0, _key_before_query(BQ, strict=False)),
                (self.d0 + 1, BK, _key_before_query(BQ - BK, strict=False))]

    def scores(self, descs):
        return [_dot(_key_block(self.k_ref, j), self.qT_ref[:, q0:]) for j, q0, _ in descs]

    def soft_group(self, descs, us):
        us = [u if mask is None else jnp.where(mask, u, -jnp.inf)
              for (_, _, mask), u in zip(descs, us)]
        mx = None
        for (_, q0, _), u in zip(descs, us):
            cm = _widen(jnp.max(u, axis=0, keepdims=True), q0, -jnp.inf)
            mx = cm if mx is None else jnp.maximum(mx, cm)
        ct = self.ct_ref[...]
        m_old = self.m_ref[...]
        m_new = jnp.maximum(m_old, mx + ct)
        off = ct - m_new
        alpha = jnp.exp2(m_old - m_new)
        l = alpha * self.l_ref[...]
        acc = alpha * self.acc_ref[...]
        for (j, q0, _), u in zip(descs, us):
            p = jnp.exp2(u + off[:, q0:])
            pv, psum = self.pv_and_colsum(j, p.astype(BF16))
            l = l + _widen(psum, q0, 0.0)
            acc = acc + _widen(pv, q0, 0.0)
        self.m_ref[...] = m_new
        self.l_ref[...] = l
        self.acc_ref[...] = acc

    def pv_and_colsum(self, j, p):
        ones = jnp.ones((ONES_ROWS, BK), BF16)
        out = _dot(jnp.concatenate([self.vT_ref[j], ones], axis=0), p)
        return out[:HEAD_DIM, :], out[HEAD_DIM:HEAD_DIM + 1, :]

    def batch(self, groups):
        scores = [self.scores(descs) for descs in groups]
        for descs, us in zip(groups, scores):
            self.soft_group(descs, us)

    def step(self, score_t=None, soft_t=None, pv_t=None):
        if score_t is not None:
            u = _dot(_key_block(self.k_ref, score_t[0]), self.qT_ref[...])
        if pv_t is not None:
            pv_set = self.sets[pv_t[1]]
            pv, psum = self.pv_and_colsum(pv_t[0], pv_set[2][...])
            alpha = pv_set[3][...]
            self.acc_ref[...] = alpha * self.acc_ref[...] + pv
            self.l_ref[...] = alpha * self.l_ref[...] + psum
        if score_t is not None:
            us_ref, mxs_ref, _, _ = self.sets[score_t[1]]
            us_ref[...] = u
            mxs_ref[...] = jnp.max(u, axis=0, keepdims=True)
        if soft_t is not None:
            u_ref, mx_ref, p_ref, al_ref = self.sets[soft_t[1]]
            ct = self.ct_ref[...]
            m_old = self.m_ref[...]
            m_new = jnp.maximum(m_old, mx_ref[...] + ct)
            p = jnp.exp2(u_ref[...] + (ct - m_new))
            alpha = jnp.exp2(m_old - m_new)
            self.m_ref[...] = m_new
            p_ref[...] = p.astype(BF16)
            al_ref[...] = alpha

    def alive(self, t):
        j = jnp.maximum(self.key_tile(t), 0)
        kmax = self.kpre_ref[self.h, j]
        c_end = self.cend_ref[self.h, j]
        bound = (self.qn_ref[...] * (kmax * FX_NORM_SLACK)
                 + (self.ct_ref[...] - c_end) - self.m_ref[...])
        return jnp.max(bound) >= -(FX_DEAD_LOG2 + 1e-5 * jnp.abs(c_end))

    def more_after_four(self):
        return jnp.logical_and(self.alive(4), self.n_tiles > 4)

    def rest(self, enter):
        key_tile = self.key_tile

        @pl.when(enter)
        def _():
            self.step(score_t=(key_tile(4), 0))
            self.step(score_t=(key_tile(5), 1), soft_t=(key_tile(4), 0))

            def cond(carry):
                return jnp.logical_and(carry[0] < self.n_tiles, carry[1] > 0)

            def body(carry):
                t = carry[0]
                next_alive = self.alive(t + 2).astype(jnp.int32)
                self.step(score_t=(key_tile(t), 0), soft_t=(key_tile(t - 1), 1),
                          pv_t=(key_tile(t - 2), 0))
                self.step(score_t=(key_tile(t + 1), 1), soft_t=(key_tile(t), 0),
                          pv_t=(key_tile(t - 1), 1))
                return t + 2, next_alive

            t_end, _ = lax.while_loop(
                cond, body, (jnp.int32(6), self.alive(6).astype(jnp.int32)))
            self.step(soft_t=(key_tile(t_end - 1), 1), pv_t=(key_tile(t_end - 2), 0))
            self.step(pv_t=(key_tile(t_end - 1), 1))


Q_BLOCKS_PER_STEP = 2


def _fast_tiles(sb, fx):
    fx.set_query_norms()
    s_descs = sb.diag() + [(2, 0, BK, None)]
    f_groups = [fx.diag(), [(fx.key_tile(2), 0, None), (fx.key_tile(3), 0, None)]]
    s_z = sb.scores(s_descs)
    f_u = [fx.scores(descs) for descs in f_groups]
    parts = sb.splits(s_descs, s_z)
    csums = [sb.suffix_sums(*parts[0])]
    fx.soft_group(f_groups[0], f_u[0])
    csums += [sb.suffix_sums(hi, lo) for hi, lo in parts[1:]]
    fx.soft_group(f_groups[1], f_u[1])
    sb.finish(s_descs, s_z, csums)


def _remaining_tiles(sb, fx):
    late_alive = sb.alive_in(BK, BQ)
    early_alive = sb.alive_in(0, BK)
    fx_more = fx.more_after_four()

    @pl.when(late_alive)
    def _():
        sb.batch([(2, BK, BQ, None)])

    sb.rest(jnp.logical_or(late_alive, early_alive))
    fx.rest(fx_more)


def _attn_kernel(kpre_ref, cend_ref, qs_ref, qf_ref, ks_ref, kf_ref, vs_ref, vf_ref,
                 tri_ref, ct_ref, gs_ref, gf_ref, os_ref, of_ref, *scratch):
    h = pl.program_id(0)
    step = pl.program_id(1)
    sb_sets = (scratch[0:5], scratch[5:10])
    fx_sets = (scratch[10:14], scratch[14:18])
    state = scratch[18:]
    blocks = []
    for slot in range(Q_BLOCKS_PER_STEP):
        i = step * Q_BLOCKS_PER_STEP + slot
        lanes = pl.ds(slot * BQ, BQ)
        sb_acc, sb_run, fx_acc, fx_m, fx_l, fx_qn = state[6 * slot:6 * slot + 6]
        sb = _StickBreaking(i, qs_ref.at[:, lanes], ks_ref, vs_ref, tri_ref, sb_sets,
                            sb_acc, sb_run)
        fx = _Forgetting(h, i, kpre_ref, cend_ref, qf_ref.at[:, lanes], kf_ref, vf_ref,
                         ct_ref.at[:, lanes], fx_sets, fx_acc, fx_m, fx_l, fx_qn)
        sb.init()
        fx.init()
        blocks.append((sb, fx))

    @pl.when(step == 0)
    def _():
        sb0, fx0 = blocks[0]
        sb0.batch(sb0.diag())
        fx0.batch([fx0.diag()])
        for sb, fx in blocks[1:]:
            _fast_tiles(sb, fx)
            _remaining_tiles(sb, fx)

    @pl.when(step > 0)
    def _():
        for sb, fx in blocks:
            _fast_tiles(sb, fx)
            _remaining_tiles(sb, fx)

    for slot, (sb, fx) in enumerate(blocks):
        lanes = slice(slot * BQ, (slot + 1) * BQ)
        os_ref[:, lanes] = _head_rmsnorm_T(sb.acc_ref[...], gs_ref[...]).astype(os_ref.dtype)
        of_ref[:, lanes] = _head_rmsnorm_T(fx.acc_ref[...] / fx.l_ref[...],
                                           gf_ref[...]).astype(of_ref.dtype)


def _attn_scratch():
    tile_f32 = pltpu.VMEM((BK, BQ), F32)
    tile_bf16 = pltpu.VMEM((BK, BQ), BF16)
    row = pltpu.VMEM((1, BQ), F32)
    acc = pltpu.VMEM((HEAD_DIM, BQ), F32)
    sb_set = [tile_f32, tile_bf16, tile_bf16, tile_bf16, row]
    fx_set = [tile_f32, row, tile_bf16, row]
    per_block = [acc, row, acc, row, row, row]
    return sb_set + sb_set + fx_set + fx_set + per_block * Q_BLOCKS_PER_STEP


def _attention(qT, k, vT, tri, ct, kpre, cend, g_sb_col, g_fx_col):
    s = k.shape[0]
    bq = BQ * Q_BLOCKS_PER_STEP
    nkb = s // BK
    smem_spec = pl.BlockSpec(memory_space=pltpu.SMEM)
    out_spec = pl.BlockSpec((HEAD_DIM, bq), lambda h, i: (h, i))
    out_shape = jax.ShapeDtypeStruct((N_SB * HEAD_DIM, s), BF16)
    return pl.pallas_call(
        _attn_kernel,
        grid=(N_SB, s // bq),
        in_specs=[
            smem_spec, smem_spec,
            pl.BlockSpec((HEAD_PAD, bq), lambda h, i: (h, i)),
            pl.BlockSpec((HEAD_PAD, bq), lambda h, i: (N_SB + h, i)),
            pl.BlockSpec((s, HEAD_PAD), lambda h, i: (0, h // 2)),
            pl.BlockSpec((s, HEAD_PAD), lambda h, i: (0, SB_K_WIDTH // HEAD_PAD + h)),
            pl.BlockSpec((nkb, HEAD_DIM, BK), lambda h, i: (0, h, 0)),
            pl.BlockSpec((nkb, HEAD_DIM, BK), lambda h, i: (0, N_SB + h, 0)),
            pl.BlockSpec((BK, BK), lambda h, i: (0, 0)),
            pl.BlockSpec((None, 1, bq), lambda h, i: (h, 0, i)),
            pl.BlockSpec((HEAD_DIM, 1), lambda h, i: (h, 0)),
            pl.BlockSpec((HEAD_DIM, 1), lambda h, i: (h, 0)),
        ],
        out_specs=[out_spec, out_spec],
        out_shape=[out_shape, out_shape],
        scratch_shapes=_attn_scratch(),
        compiler_params=pltpu.CompilerParams(
            dimension_semantics=("arbitrary", "arbitrary"),
            vmem_limit_bytes=VMEM_LIMIT),
        name="attn",
    )(kpre, cend, qT, qT, k, k, vT, vT, tri, ct, g_sb_col, g_fx_col)


def _layer_norm(v, g, b):
    mu = jnp.mean(v, axis=-1, keepdims=True)
    vc = v - mu
    var = jnp.mean(vc * vc, axis=-1, keepdims=True)
    return vc * lax.rsqrt(var + LN_EPS) * g + b


def _ffn_kernel(alpha, x_ref, osb_ref, ofx_ref, woa_ref, wob_ref,
                g1_ref, b1_ref, g2_ref, b2_ref, wgu_ref, wd_ref,
                y_ref, act_ref):
    d_ff = wd_ref.shape[0]
    mix = (lax.dot_general(osb_ref[...], woa_ref[...], _TN, preferred_element_type=F32)
           + lax.dot_general(ofx_ref[...], wob_ref[...], _TN, preferred_element_type=F32))
    h1 = _layer_norm(alpha * x_ref[...] + mix, g1_ref[...], b1_ref[...])
    h1b = h1.astype(BF16)
    for c in range(d_ff // FF_CHUNK):
        lo, hi = c * FF_CHUNK, (c + 1) * FF_CHUNK
        gate = jnp.dot(h1b, wgu_ref[:, lo:hi], preferred_element_type=F32)
        up = jnp.dot(h1b, wgu_ref[:, d_ff + lo:d_ff + hi], preferred_element_type=F32)
        act_ref[:, lo:hi] = (gate * jax.nn.sigmoid(gate) * up).astype(BF16)
    ff = jnp.dot(act_ref[...], wd_ref[...], preferred_element_type=F32)
    y_ref[...] = _layer_norm(alpha * h1 + ff, g2_ref[...], b2_ref[...])


def _out_ffn(alpha, x2, oT_sb, oT_fx, wo_a, wo_b, g1, b1, g2, b2, wgu, wd):
    s, dm = x2.shape
    bs = BLOCK_ROWS
    d_ff = wd.shape[0]
    half = oT_sb.shape[0]
    return pl.pallas_call(
        functools.partial(_ffn_kernel, alpha),
        grid=(s // bs,),
        in_specs=[
            pl.BlockSpec((bs, dm), lambda i: (i, 0)),
            pl.BlockSpec((half, bs), lambda i: (0, i)),
            pl.BlockSpec((half, bs), lambda i: (0, i)),
            _const_spec(wo_a.shape),
            _const_spec(wo_b.shape),
            _const_spec(g1.shape),
            _const_spec(b1.shape),
            _const_spec(g2.shape),
            _const_spec(b2.shape),
            _const_spec(wgu.shape),
            _const_spec(wd.shape),
        ],
        out_specs=pl.BlockSpec((bs, dm), lambda i: (i, 0)),
        out_shape=jax.ShapeDtypeStruct((s, dm), F32),
        scratch_shapes=[pltpu.VMEM((bs, d_ff), BF16)],
        compiler_params=pltpu.CompilerParams(
            dimension_semantics=("arbitrary",), vmem_limit_bytes=VMEM_LIMIT),
        name="out_ffn",
    )(x2, oT_sb, oT_fx, wo_a, wo_b, g1, b1, g2, b2, wgu, wd)


def _layer(x2, w_in, b_f, g_sb, g_fox, w_out, ln1_g, ln1_b, ln2_g, ln2_b,
           w_gate_up, w_down, alpha):
    assert NSUB == 2
    s, dm = x2.shape
    sbw = N_SB * HEAD_DIM
    fxw = N_FX * HEAD_DIM
    q_sb, k_sb, v_sb = w_in[:, :sbw], w_in[:, sbw:2 * sbw], w_in[:, 2 * sbw:3 * sbw]
    o = 3 * sbw
    q_fx, k_fx, v_fx = w_in[:, o:o + fxw], w_in[:, o + fxw:o + 2 * fxw], w_in[:, o + 2 * fxw:o + 3 * fxw]
    wf = w_in[:, o + 3 * fxw:]

    wqT = jnp.concatenate([q_sb, q_fx], axis=1).T.astype(BF16)
    wks = k_sb.astype(BF16)
    wkf = jnp.pad(k_fx.reshape(dm, N_FX, HEAD_DIM), ((0, 0), (0, 0), (0, HEAD_PAD - HEAD_DIM)))
    wkf = wkf.reshape(dm, N_FX * HEAD_PAD).astype(BF16)
    wvT = jnp.concatenate([v_sb, v_fx], axis=1).T.astype(BF16)

    wf_hi = wf.astype(BF16)
    wf_lo = (wf - wf_hi.astype(F32)).astype(BF16)
    wf2 = jnp.concatenate([wf_hi, wf_lo], axis=1)

    qT, k, vT, c, kpre, cend = _projection(x2, wqT, wks, wkf, wvT, wf2,
                                           b_f.reshape(1, N_FX))

    srow = lax.broadcasted_iota(jnp.int32, (BK, BK), 0)
    jcol = lax.broadcasted_iota(jnp.int32, (BK, BK), 1)
    tri = (jcol >= srow).astype(BF16)
    ct = c.T.reshape(N_FX, 1, s)
    kpre = kpre.reshape(s // BK, N_FX).T
    cend = cend.reshape(s // BK, N_FX).T

    oT_sb, oT_fx = _attention(qT, k, vT, tri, ct, kpre, cend,
                              g_sb.reshape(sbw, 1), g_fox.reshape(fxw, 1))

    wo = w_out.astype(BF16)
    return _out_ffn(
        alpha, x2, oT_sb, oT_fx, wo[:sbw], wo[sbw:],
        ln1_g.reshape(1, dm), ln1_b.reshape(1, dm),
        ln2_g.reshape(1, dm), ln2_b.reshape(1, dm),
        w_gate_up.astype(BF16), w_down.astype(BF16))


def kernel(x, w_in, b_f, g_sb, g_fox, w_out, ln1_g, ln1_b, ln2_g, ln2_b, w_gate_up, w_down):
    batch, s, dm = x.shape
    depth = w_in.shape[0]
    alpha = (2 * depth) ** 0.25
    outs = []
    for b in range(batch):
        h = x[b]
        for l in range(depth):
            h = _layer(h, w_in[l], b_f[l], g_sb[l], g_fox[l], w_out[l],
                       ln1_g[l], ln1_b[l], ln2_g[l], ln2_b[l],
                       w_gate_up[l], w_down[l], alpha)
        outs.append(h)
    return outs[0][None] if batch == 1 else jnp.stack(outs, axis=0)
```
